```python
import math
import jax, jax.numpy as jnp
from jax import lax
import numpy as np


D_MODEL = 1024
BATCH = 4
SEQ = 8192
DEPTH = 1

MLSTM_HEADS = 4
MLSTM_WIDTH = D_MODEL
MLSTM_VDIM = MLSTM_WIDTH // MLSTM_HEADS
MLSTM_QKDIM = MLSTM_VDIM // 2
MLSTM_CONV = 4
MLSTM_CHUNK = 64
S5_WIDTH = D_MODEL // 2
S5_GROUP = 16
S5_GROUPS = S5_WIDTH // S5_GROUP
S5_STATE = 64
S5_DT_MIN = 1e-3
S5_DT_MAX = 1e-1
FFN_HIDDEN = ((8 * D_MODEL // 3 + 127) // 128) * 128
FFN_CONV = 3
ALPHA = (2.0 * DEPTH) ** 0.25
BETA = (8.0 * DEPTH) ** -0.25
LN_EPS = 1e-5
IN_SIZES = (MLSTM_WIDTH, MLSTM_WIDTH, MLSTM_HEADS, MLSTM_HEADS, S5_WIDTH, D_MODEL, D_MODEL)
IN_WIDTH = sum(IN_SIZES)
IN_SPLITS = tuple(int(s) for s in np.cumsum(IN_SIZES)[:-1])
F_OFF = 2 * MLSTM_WIDTH + MLSTM_HEADS

kernel_name = 'hybrid_mlstm_s5_convglu_deepnorm_adaln'


def _standardize(x):
    xf = x.astype(jnp.float32)
    mu = jnp.mean(xf, axis=-1, keepdims=True)
    var = jnp.mean(jnp.square(xf - mu), axis=-1, keepdims=True)
    return ((xf - mu) * lax.rsqrt(var + LN_EPS)).astype(x.dtype)


def _layer_norm(x, gain, bias):
    return _standardize(x) * gain + bias


def _causal_dwconv(x, w, b):
    K = w.shape[0]
    S = x.shape[1]
    xp = jnp.pad(x, ((0, 0), (K - 1, 0), (0, 0)))
    y = b
    for j in range(K):
        y = y + w[j] * xp[:, j:j + S]
    return y


def _mlstm_chunkwise(q, k, v, i_pre, f_pre):
    Bsz, H, S, dk = q.shape
    dv = v.shape[-1]
    L = MLSTM_CHUNK
    nc = S // L
    f32 = jnp.float32
    q = q.astype(f32) * (MLSTM_QKDIM ** -0.5)
    k = k.astype(f32)
    v = v.astype(f32)
    ig = i_pre.astype(f32)
    logf = jax.nn.log_sigmoid(f_pre.astype(f32))

    def to_chunks(a):
        return jnp.moveaxis(a.reshape((Bsz, H, nc, L) + a.shape[3:]), 2, 0)

    causal = jnp.tril(jnp.ones((L, L), dtype=bool))

    def step(carry, xs):
        C, n, m = carry
        qb, kb, vb, ib, fb = xs
        b = jnp.cumsum(fb, axis=-1)
        D = b[..., :, None] - b[..., None, :] + ib[..., None, :]
        D = jnp.where(causal, D, -jnp.inf)
        inter = b + m[..., None]
        m_t = jnp.maximum(inter, jnp.max(D, axis=-1))
        wmat = jnp.exp(D - m_t[..., None])
        sc_inter = jnp.exp(inter - m_t)
        s = jnp.einsum('bhtd,bhsd->bhts', qb, kb) * wmat
        num = jnp.einsum('bhts,bhsv->bhtv', s, vb) + sc_inter[..., None] * jnp.einsum('bhtd,bhdv->bhtv', qb, C)
        den = jnp.sum(s, axis=-1) + sc_inter * jnp.einsum('bhtd,bhd->bht', qb, n)
        h = num / jnp.maximum(jnp.abs(den), jnp.exp(-m_t))[..., None]
        b_last = b[..., -1]
        g = b_last[..., None] - b + ib
        m_new = jnp.maximum(b_last + m, jnp.max(g, axis=-1))
        wk = jnp.exp(g - m_new[..., None])
        decay = jnp.exp(b_last + m - m_new)
        kw = kb * wk[..., None]
        C_new = decay[..., None, None] * C + jnp.einsum('bhsd,bhsv->bhdv', kw, vb)
        n_new = decay[..., None] * n + jnp.sum(kw, axis=2)
        return (C_new, n_new, m_new), h

    init = (jnp.zeros((Bsz, H, dk, dv), f32), jnp.zeros((Bsz, H, dk), f32), jnp.zeros((Bsz, H), f32))
    _, hc = lax.scan(step, init, (to_chunks(q), to_chunks(k), to_chunks(v), to_chunks(ig), to_chunks(logf)))
    return jnp.moveaxis(hc, 0, 2).reshape(Bsz, H, S, dv)


def _s5_ssm(u, lam_re, lam_im, log_dt, b_re, b_im, c_re, c_im, d):
    Bsz, S, W = u.shape
    f32 = jnp.float32
    uf = u.astype(f32)
    lam_re = lam_re.astype(f32)
    lam_im = lam_im.astype(f32)
    dt = jnp.exp(log_dt.astype(f32))[:, None]
    mag = jnp.exp(lam_re * dt)
    ar = mag * jnp.cos(lam_im * dt)
    ai = mag * jnp.sin(lam_im * dt)
    den = lam_re * lam_re + lam_im * lam_im
    zr = ((ar - 1.0) * lam_re + ai * lam_im) / den
    zi = (ai * lam_re - (ar - 1.0) * lam_im) / den
    bbr = zr[..., None] * b_re - zi[..., None] * b_im
    bbi = zr[..., None] * b_im + zi[..., None] * b_re
    ug = uf.reshape(Bsz, S, S5_GROUPS, S5_GROUP)
    bu_r = jnp.einsum('bsgc,gpc->bsgp', ug, bbr)
    bu_i = jnp.einsum('bsgc,gpc->bsgp', ug, bbi)
    a_r = jnp.broadcast_to(ar, bu_r.shape)
    a_i = jnp.broadcast_to(ai, bu_r.shape)

    def combine(e1, e2):
        a1r, a1i, b1r, b1i = e1
        a2r, a2i, b2r, b2i = e2
        return (a1r * a2r - a1i * a2i,
                a1r * a2i + a1i * a2r,
                a2r * b1r - a2i * b1i + b2r,
                a2r * b1i + a2i * b1r + b2i)

    _, _, xr, xi = lax.associative_scan(combine, (a_r, a_i, bu_r, bu_i), axis=1)
    y = jnp.einsum('gcp,bsgp->bsgc', c_re, xr) - jnp.einsum('gcp,bsgp->bsgc', c_im, xi)
    y = y.reshape(Bsz, S, W) + d * uf
    return y.astype(u.dtype)


def _hybrid_mixer(h, w_in, b_in, w_mlstm_conv, b_mlstm_conv, w_mlstm_q, w_mlstm_k, mlstm_norm_gain,
                  w_mlstm_down, s5_lam_re, s5_lam_im, s5_log_dt, s5_b_re, s5_b_im, s5_c_re, s5_c_im,
                  s5_d, w_s5_glu, w_mix_out):
    Bsz, S, _ = h.shape
    proj = h @ w_in + b_in
    xm, om, ip, fp, us, ga, gb = jnp.split(proj, IN_SPLITS, axis=-1)
    xc = jax.nn.silu(_causal_dwconv(xm, w_mlstm_conv, b_mlstm_conv)).reshape(Bsz, S, MLSTM_HEADS, MLSTM_VDIM)
    q = jnp.einsum('bshc,hcd->bhsd', xc, w_mlstm_q)
    k = jnp.einsum('bshc,hcd->bhsd', xc, w_mlstm_k)
    v = xm.reshape(Bsz, S, MLSTM_HEADS, MLSTM_VDIM).transpose(0, 2, 1, 3)
    hm = _mlstm_chunkwise(q, k, v, ip.transpose(0, 2, 1), fp.transpose(0, 2, 1))
    hm = _standardize(hm).transpose(0, 2, 1, 3).reshape(Bsz, S, MLSTM_WIDTH).astype(h.dtype)
    y_a = (hm * mlstm_norm_gain * jax.nn.sigmoid(om)) @ w_mlstm_down
    ys = jax.nn.gelu(_s5_ssm(us, s5_lam_re, s5_lam_im, s5_log_dt, s5_b_re, s5_b_im, s5_c_re, s5_c_im, s5_d))
    val, gate = jnp.split(ys @ w_s5_glu, 2, axis=-1)
    y_b = val * jax.nn.sigmoid(gate)
    y = jax.nn.sigmoid(ga) * y_a + jax.nn.sigmoid(gb) * y_b
    return y @ w_mix_out


def _conv_glu_ffn(h, w_ffn_up, w_ffn_conv, b_ffn_conv, w_ffn_down):
    val, gate = jnp.split(h @ w_ffn_up, 2, axis=-1)
    gate = jax.nn.gelu(_causal_dwconv(gate, w_ffn_conv, b_ffn_conv))
    return (gate * val) @ w_ffn_down


def setup_inputs(seed: int = 0) -> dict:
    key = jax.random.key(seed)
    ks = iter(jax.random.split(key, 40))
    f32 = jnp.float32
    L = DEPTH

    def nrm(shape, scale):
        return scale * jax.random.normal(next(ks), shape, f32)

    x = nrm((BATCH, SEQ, D_MODEL), 1.0)
    c = nrm((BATCH, D_MODEL), 1.0)
    w_ada = nrm((L, D_MODEL, 6 * D_MODEL), 0.5 * D_MODEL ** -0.5)
    b_ada = nrm((L, 6 * D_MODEL), 0.02)
    w_in = nrm((L, D_MODEL, IN_WIDTH), D_MODEL ** -0.5)
    f_bias = jnp.linspace(3.0, 6.0, MLSTM_HEADS, dtype=f32)
    b_in = nrm((L, IN_WIDTH), 0.02).at[:, F_OFF:F_OFF + MLSTM_HEADS].add(f_bias)
    w_mlstm_conv = nrm((L, MLSTM_CONV, MLSTM_WIDTH), MLSTM_CONV ** -0.5)
    b_mlstm_conv = nrm((L, MLSTM_WIDTH), 0.02)
    w_mlstm_q = nrm((L, MLSTM_HEADS, MLSTM_VDIM, MLSTM_QKDIM), MLSTM_VDIM ** -0.5)
    w_mlstm_k = nrm((L, MLSTM_HEADS, MLSTM_VDIM, MLSTM_QKDIM), MLSTM_VDIM ** -0.5)
    mlstm_norm_gain = 1.0 + nrm((L, MLSTM_WIDTH), 0.02)
    w_mlstm_down = nrm((L, MLSTM_WIDTH, D_MODEL), MLSTM_WIDTH ** -0.5)
    s5_lam_re = -0.5 + nrm((L, S5_GROUPS, S5_STATE), 0.01)
    s5_lam_im = math.pi * jnp.arange(S5_STATE, dtype=f32) + nrm((L, S5_GROUPS, S5_STATE), 0.01)
    s5_log_dt = math.log(S5_DT_MIN) + jax.random.uniform(next(ks), (L, S5_GROUPS), f32) * (math.log(S5_DT_MAX) - math.log(S5_DT_MIN))
    s5_b_re = nrm((L, S5_GROUPS, S5_STATE, S5_GROUP), (2.0 * S5_GROUP) ** -0.5)
    s5_b_im = nrm((L, S5_GROUPS, S5_STATE, S5_GROUP), (2.0 * S5_GROUP) ** -0.5)
    s5_c_re = nrm((L, S5_GROUPS, S5_GROUP, S5_STATE), (2.0 * S5_STATE) ** -0.5)
    s5_c_im = nrm((L, S5_GROUPS, S5_GROUP, S5_STATE), (2.0 * S5_STATE) ** -0.5)
    s5_d = nrm((L, S5_WIDTH), 1.0)
    w_s5_glu = nrm((L, S5_WIDTH, 2 * D_MODEL), S5_WIDTH ** -0.5)
    w_mix_out = nrm((L, D_MODEL, D_MODEL), BETA * D_MODEL ** -0.5)
    ln1_gain = 1.0 + nrm((L, D_MODEL), 0.02)
    ln1_bias = nrm((L, D_MODEL), 0.02)
    w_ffn_up = nrm((L, D_MODEL, 2 * FFN_HIDDEN), D_MODEL ** -0.5)
    w_ffn_conv = nrm((L, FFN_CONV, FFN_HIDDEN), FFN_CONV ** -0.5)
    b_ffn_conv = nrm((L, FFN_HIDDEN), 0.02)
    w_ffn_down = nrm((L, FFN_HIDDEN, D_MODEL), BETA * FFN_HIDDEN ** -0.5)
    ln2_gain = 1.0 + nrm((L, D_MODEL), 0.02)
    ln2_bias = nrm((L, D_MODEL), 0.02)
    return {'x': x, 'c': c, 'w_ada': w_ada, 'b_ada': b_ada, 'w_in': w_in, 'b_in': b_in,
            'w_mlstm_conv': w_mlstm_conv, 'b_mlstm_conv': b_mlstm_conv, 'w_mlstm_q': w_mlstm_q,
            'w_mlstm_k': w_mlstm_k, 'mlstm_norm_gain': mlstm_norm_gain, 'w_mlstm_down': w_mlstm_down,
            's5_lam_re': s5_lam_re, 's5_lam_im': s5_lam_im, 's5_log_dt': s5_log_dt, 's5_b_re': s5_b_re,
            's5_b_im': s5_b_im, 's5_c_re': s5_c_re, 's5_c_im': s5_c_im, 's5_d': s5_d, 'w_s5_glu': w_s5_glu,
            'w_mix_out': w_mix_out, 'ln1_gain': ln1_gain, 'ln1_bias': ln1_bias, 'w_ffn_up': w_ffn_up,
            'w_ffn_conv': w_ffn_conv, 'b_ffn_conv': b_ffn_conv, 'w_ffn_down': w_ffn_down,
            'ln2_gain': ln2_gain, 'ln2_bias': ln2_bias}


def reference(x, c, w_ada, b_ada, w_in, b_in, w_mlstm_conv, b_mlstm_conv, w_mlstm_q, w_mlstm_k,
              mlstm_norm_gain, w_mlstm_down, s5_lam_re, s5_lam_im, s5_log_dt, s5_b_re, s5_b_im,
              s5_c_re, s5_c_im, s5_d, w_s5_glu, w_mix_out, ln1_gain, ln1_bias, w_ffn_up, w_ffn_conv,
              b_ffn_conv, w_ffn_down, ln2_gain, ln2_bias):
    c_act = jax.nn.silu(c)
    for l in range(DEPTH):
        mod = c_act @ w_ada[l] + b_ada[l]
        sh1, sc1, g1, sh2, sc2, g2 = [m[:, None, :] for m in jnp.split(mod, 6, axis=-1)]
        h = _standardize(x) * (1.0 + sc1) + sh1
        mix = _hybrid_mixer(h, w_in[l], b_in[l], w_mlstm_conv[l], b_mlstm_conv[l], w_mlstm_q[l],
                            w_mlstm_k[l], mlstm_norm_gain[l], w_mlstm_down[l], s5_lam_re[l],
                            s5_lam_im[l], s5_log_dt[l], s5_b_re[l], s5_b_im[l], s5_c_re[l],
                            s5_c_im[l], s5_d[l], w_s5_glu[l], w_mix_out[l])
        x = _layer_norm(ALPHA * x + (1.0 + g1) * mix, ln1_gain[l], ln1_bias[l])
        h = _standardize(x) * (1.0 + sc2) + sh2
        f = _conv_glu_ffn(h, w_ffn_up[l], w_ffn_conv[l], b_ffn_conv[l], w_ffn_down[l])
        x = _layer_norm(ALPHA * x + (1.0 + g2) * f, ln2_gain[l], ln2_bias[l])
    return x
```

```python
import functools
import math

import jax
import jax.numpy as jnp
from jax import lax
from jax.experimental import pallas as pl
from jax.experimental.pallas import tpu as pltpu

F32 = jnp.float32
BF16 = jnp.bfloat16
HIGHEST = lax.Precision.HIGHEST

LN_EPS = 1e-5
MLSTM_HEADS = 4
MLSTM_CONV = 4
FFN_CONV = 3
S5_GROUP = 16
S5_STATE = 64
LANES = 128
SUBLANES = 8
VMEM_LIMIT_BYTES = 56 * 1024 * 1024

NT_DIMS = (((1,), (1,)), ((), ()))
TN_DIMS = (((0,), (0,)), ((), ()))


def _standardize(x):
    mu = jnp.mean(x, axis=-1, keepdims=True)
    xc = x - mu
    var = jnp.mean(xc * xc, axis=-1, keepdims=True)
    return xc * lax.rsqrt(var + LN_EPS)


def _sigmoid(x):
    return 1.0 / (1.0 + jnp.exp(-x))


def _gelu_tanh(x):
    return 0.5 * x * (1.0 + jnp.tanh(math.sqrt(2.0 / math.pi) * (x + 0.044715 * (x * x * x))))


def _params(*semantics):
    return pltpu.CompilerParams(dimension_semantics=semantics, vmem_limit_bytes=VMEM_LIMIT_BYTES)


def _adaln_kernel(c_ref, w_ref, b_ref, o_ref):
    c = c_ref[...]
    ca = c * _sigmoid(c)
    o_ref[...] = jnp.dot(ca, w_ref[...], precision=HIGHEST, preferred_element_type=F32) + b_ref[...]


def _adaln(c, w, b):
    bsz, d = c.shape
    n = w.shape[1]
    return pl.pallas_call(
        _adaln_kernel,
        grid=(n // d,),
        in_specs=[pl.BlockSpec((bsz, d), lambda j: (0, 0)),
                  pl.BlockSpec((d, d), lambda j: (0, j)),
                  pl.BlockSpec((1, d), lambda j: (0, j))],
        out_specs=pl.BlockSpec((bsz, d), lambda j: (0, j)),
        out_shape=jax.ShapeDtypeStruct((bsz, n), F32),
        compiler_params=_params("arbitrary"),
        name="adaln",
    )(c, w, b.reshape(1, n))


def _inproj_kernel(x_ref, mod_ref, wtok_ref, btok_ref, wt_ref, bt_ref,
                   xm_ref, som_ref, sga_ref, ust_ref, sgbt_ref, gt_ref, *, d, s5w):
    x = x_ref[...]
    h = (_standardize(x) * (1.0 + mod_ref[1:2, :]) + mod_ref[0:1, :]).astype(BF16)
    p = jnp.dot(h, wtok_ref[:, 0:d], preferred_element_type=F32) + btok_ref[:, 0:d]
    xm_ref[...] = p.astype(BF16)
    p = jnp.dot(h, wtok_ref[:, d:2 * d], preferred_element_type=F32) + btok_ref[:, d:2 * d]
    som_ref[...] = _sigmoid(p).astype(BF16)
    p = jnp.dot(h, wtok_ref[:, 2 * d:3 * d], preferred_element_type=F32) + btok_ref[:, 2 * d:3 * d]
    sga_ref[...] = _sigmoid(p).astype(BF16)
    pt = lax.dot_general(wt_ref[0:s5w, :], h, NT_DIMS, preferred_element_type=F32) + bt_ref[0:s5w, :]
    ust_ref[...] = pt.astype(BF16)
    pt = lax.dot_general(wt_ref[s5w:s5w + d, :], h, NT_DIMS, preferred_element_type=F32) + bt_ref[s5w:s5w + d, :]
    sgbt_ref[...] = _sigmoid(pt).astype(BF16)
    ng = 2 * MLSTM_HEADS
    pt = lax.dot_general(wt_ref[s5w + d:s5w + d + ng, :], h, NT_DIMS, preferred_element_type=F32)
    gt_ref[...] = pt + bt_ref[s5w + d:s5w + d + ng, :]


def _inproj(x, mod, wtok, btok, wt, bt, *, tm, s5w):
    bsz, s, d = x.shape
    nt = wt.shape[0]
    ng = 2 * MLSTM_HEADS
    tok = lambda b, i: (b, i, 0)
    chan = lambda b, i: (b, 0, i)
    const = lambda b, i: (0, 0)
    return pl.pallas_call(
        functools.partial(_inproj_kernel, d=d, s5w=s5w),
        grid=(bsz, s // tm),
        in_specs=[pl.BlockSpec((None, tm, d), tok),
                  pl.BlockSpec((None, 6, d), lambda b, i: (b, 0, 0)),
                  pl.BlockSpec((d, 3 * d), const),
                  pl.BlockSpec((1, 3 * d), const),
                  pl.BlockSpec((nt, d), const),
                  pl.BlockSpec((nt, 1), const)],
        out_specs=[pl.BlockSpec((None, tm, d), tok),
                   pl.BlockSpec((None, tm, d), tok),
                   pl.BlockSpec((None, tm, d), tok),
                   pl.BlockSpec((None, s5w, tm), chan),
                   pl.BlockSpec((None, d, tm), chan),
                   pl.BlockSpec((None, ng, tm), chan)],
        out_shape=[jax.ShapeDtypeStruct((bsz, s, d), BF16),
                   jax.ShapeDtypeStruct((bsz, s, d), BF16),
                   jax.ShapeDtypeStruct((bsz, s, d), BF16),
                   jax.ShapeDtypeStruct((bsz, s5w, s), BF16),
                   jax.ShapeDtypeStruct((bsz, d, s), BF16),
                   jax.ShapeDtypeStruct((bsz, ng, s), F32)],
        compiler_params=_params("arbitrary", "arbitrary"),
        name="inproj",
    )(x, mod, wtok, btok, wt, bt)


def _lane_cumsum(x):
    n = x.shape[-1]
    lane = lax.broadcasted_iota(jnp.int32, x.shape, x.ndim - 1)
    sh = 1
    while sh < n:
        x = x + jnp.where(lane >= sh, pltpu.roll(x, sh, x.ndim - 1), 0.0)
        sh *= 2
    return x


def _mlstm_kernel(xm_ref, som_ref, gt_ref, wconv_ref, bconv_ref, wq_ref, wk_ref, gain_ref,
                  out_ref, xbuf, c_ref, n_ref, m_ref, *, chunk, dv, dk):
    L = chunk
    H = MLSTM_HEADS

    @pl.when(pl.program_id(1) == 0)
    def _():
        xbuf[L:L + SUBLANES, :] = jnp.zeros((SUBLANES, xbuf.shape[1]), F32)
        c_ref[...] = jnp.zeros(c_ref.shape, F32)
        n_ref[...] = jnp.zeros(n_ref.shape, F32)
        m_ref[...] = jnp.zeros(m_ref.shape, F32)

    xbuf[0:SUBLANES, :] = xbuf[L:L + SUBLANES, :]
    xbuf[SUBLANES:SUBLANES + L, :] = xm_ref[...].astype(F32)

    g = gt_ref[...]
    row = lax.broadcasted_iota(jnp.int32, g.shape, 0)
    logf = jnp.minimum(g, 0.0) - jnp.log(1.0 + jnp.exp(-jnp.abs(g)))
    bcum = _lane_cumsum(jnp.where(row >= H, logf, 0.0))
    r_all = jnp.where(row >= H, bcum, g)
    eye = (lax.broadcasted_iota(jnp.int32, (L, L), 0) == lax.broadcasted_iota(jnp.int32, (L, L), 1)).astype(F32)
    c_all = lax.dot_general(eye, r_all, NT_DIMS, precision=HIGHEST, preferred_element_type=F32)

    ti = lax.broadcasted_iota(jnp.int32, (L, L), 0)
    si = lax.broadcasted_iota(jnp.int32, (L, L), 1)
    causal = si <= ti
    scale = dk ** -0.5

    for h in range(H):
        cols = slice(h * dv, (h + 1) * dv)
        ig_s = r_all[h:h + 1, :]
        b_s = r_all[H + h:H + h + 1, :]
        ig_t = c_all[:, h:h + 1]
        b_t = c_all[:, H + h:H + h + 1]
        b_last = b_s[:, L - 1:L]
        m_prev = m_ref[h, 0:1, 0:1]

        acc = bconv_ref[:, cols]
        for j in range(MLSTM_CONV):
            off = SUBLANES - (MLSTM_CONV - 1) + j
            acc = acc + wconv_ref[j:j + 1, cols] * xbuf[off:off + L, cols]
        xc = (acc * _sigmoid(acc)).astype(BF16)
        q = jnp.dot(xc, wq_ref[h], preferred_element_type=F32) * scale
        k = jnp.dot(xc, wk_ref[h], preferred_element_type=F32)
        qb = q.astype(BF16)
        kb = k.astype(BF16)
        v = xm_ref[:, cols]

        dmat = jnp.where(causal, b_t - b_s + ig_s, -jnp.inf)
        inter = b_t + m_prev
        m_t = jnp.maximum(inter, jnp.max(dmat, axis=1, keepdims=True))
        wmat = jnp.exp(dmat - m_t)
        sc_inter = jnp.exp(inter - m_t)
        s = lax.dot_general(qb, kb, NT_DIMS, preferred_element_type=F32) * wmat
        cmat = c_ref[h]
        num = (jnp.dot(s.astype(BF16), v, preferred_element_type=F32)
               + sc_inter * jnp.dot(qb, cmat.astype(BF16), preferred_element_type=F32))
        den = (jnp.sum(s, axis=1, keepdims=True)
               + sc_inter * jnp.sum(q * n_ref[h, 0:1, :], axis=1, keepdims=True))
        hh = num / jnp.maximum(jnp.abs(den), jnp.exp(-m_t))
        hm = _standardize(hh)
        out_ref[:, cols] = (hm * gain_ref[:, cols] * som_ref[:, cols].astype(F32)).astype(BF16)

        g_s = b_last - b_s + ig_s
        m_new = jnp.maximum(b_last + m_prev, jnp.max(g_s, axis=1, keepdims=True))
        wk_t = jnp.exp(b_last - b_t + ig_t - m_new)
        decay = jnp.exp(b_last + m_prev - m_new)
        kw = k * wk_t
        c_ref[h] = decay * cmat + lax.dot_general(kw.astype(BF16), v, TN_DIMS, preferred_element_type=F32)
        n_ref[h, 0:1, :] = decay * n_ref[h, 0:1, :] + jnp.sum(kw, axis=0, keepdims=True)
        m_ref[h] = jnp.broadcast_to(m_new, m_ref.shape[1:])


def _mlstm(xm, som, gt, wconv, bconv, wq, wk, gain, *, chunk):
    bsz, s, d = xm.shape
    H = MLSTM_HEADS
    dv = d // H
    dk = wq.shape[-1]
    ng = gt.shape[1]
    tok = lambda b, j: (b, j, 0)
    const2 = lambda b, j: (0, 0)
    const3 = lambda b, j: (0, 0, 0)
    return pl.pallas_call(
        functools.partial(_mlstm_kernel, chunk=chunk, dv=dv, dk=dk),
        grid=(bsz, s // chunk),
        in_specs=[pl.BlockSpec((None, chunk, d), tok),
                  pl.BlockSpec((None, chunk, d), tok),
                  pl.BlockSpec((None, ng, chunk), lambda b, j: (b, 0, j)),
                  pl.BlockSpec((MLSTM_CONV, d), const2),
                  pl.BlockSpec((1, d), const2),
                  pl.BlockSpec((H, dv, dk), const3),
                  pl.BlockSpec((H, dv, dk), const3),
                  pl.BlockSpec((1, d), const2)],
        out_specs=pl.BlockSpec((None, chunk, d), tok),
        out_shape=jax.ShapeDtypeStruct((bsz, s, d), BF16),
        scratch_shapes=[pltpu.VMEM((chunk + SUBLANES, d), F32),
                        pltpu.VMEM((H, dk, dv), F32),
                        pltpu.VMEM((H, SUBLANES, dk), F32),
                        pltpu.VMEM((H, SUBLANES, LANES), F32)],
        compiler_params=_params("arbitrary", "arbitrary"),
        name="mlstm",
    )(xm, som, gt, wconv, bconv, wq, wk, gain)


def _s5_kernel(us_ref, ldt_ref, lamr_ref, lami_ref, lamc_ref, bt_ref, cc_ref, ca_ref, cb_ref, d_ref,
               out_ref, toep_ref, wb_ref, wc_ref, *, nb, nchunk):
    P = S5_STATE
    NC = S5_GROUP
    LS = LANES
    M = nb * nchunk

    dt = jnp.exp(ldt_ref[...])
    lam_re = lamr_ref[...]
    lam_im = lami_ref[...]
    lane2 = lax.broadcasted_iota(jnp.int32, (1, 2 * P), 1)
    sgn = jnp.where(lane2 < P, -1.0, 1.0)

    mag = jnp.exp(lam_re * dt)
    ar = mag * jnp.cos(lam_im * dt)
    ai = mag * jnp.sin(lam_im * dt)
    den = lam_re * lam_re + lam_im * lam_im
    zr = ((ar - 1.0) * lam_re + ai * lam_im) / den
    zi = (ai * lam_re - (ar - 1.0) * lam_im) / den
    b1 = bt_ref[...]
    bb = zr * b1 + zi * sgn * pltpu.roll(b1, P, 1)
    bb_sw = pltpu.roll(bb, P, 1)

    cr2 = cc_ref[0]
    ci2 = cc_ref[1]
    m12 = (cr2[:, None, :] * (bb * (-sgn))[None, :, :]
           - ci2[:, None, :] * bb_sw[None, :, :]).reshape(NC * NC, 2 * P)
    lam_re_c = lamc_ref[0]
    lam_im_c = lamc_ref[1]
    tau = lax.broadcasted_iota(jnp.int32, (2 * P, LS), 1).astype(F32)
    prow = lax.broadcasted_iota(jnp.int32, (2 * P, LS), 0)
    e0 = jnp.exp(lam_re_c * dt * tau)
    ang0 = lam_im_c * dt * tau
    pr0 = e0 * jnp.cos(ang0)
    pi0 = e0 * jnp.sin(ang0)
    pstack = jnp.where(prow < P, pr0, pi0)
    kpairs = jnp.dot(m12, pstack, precision=HIGHEST, preferred_element_type=F32)

    srow = lax.broadcasted_iota(jnp.int32, (LS, LS), 0)
    tcol = lax.broadcasted_iota(jnp.int32, (LS, LS), 1)
    lower = tcol >= srow
    for c in range(NC):
        for cp in range(NC):
            r = c * NC + cp
            kv = jnp.broadcast_to(kpairs[r:r + 1, :], (LS, LS))
            tz = pltpu.roll(kv, 0, 1, stride=1, stride_axis=0)
            toep_ref[cp * LS:(cp + 1) * LS, c * LS:(c + 1) * LS] = jnp.where(lower, tz, 0.0).astype(BF16)

    srev = (LS - 1) - lax.broadcasted_iota(jnp.int32, (LS, 2 * P), 0).astype(F32)
    e1 = jnp.exp(lam_re * dt * srev)
    ang1 = lam_im * dt * srev
    r2 = e1 * jnp.cos(ang1)
    i2 = e1 * jnp.sin(ang1)
    bb_rot = bb_sw * sgn
    for cp in range(NC):
        wb_ref[cp * LS:(cp + 1) * LS, :] = (r2 * bb[cp:cp + 1, :] + i2 * bb_rot[cp:cp + 1, :]).astype(BF16)

    ar_c = jnp.exp(lam_re_c * dt) * jnp.cos(lam_im_c * dt)
    ai_c = jnp.exp(lam_re_c * dt) * jnp.sin(lam_im_c * dt)
    pr1 = pr0 * ar_c - pi0 * ai_c
    pi1 = pr0 * ai_c + pi0 * ar_c
    ca = ca_ref[...]
    cb = cb_ref[...]
    for c in range(NC):
        wc_ref[:, c * LS:(c + 1) * LS] = (ca[:, c:c + 1] * pr1 + cb[:, c:c + 1] * pi1).astype(BF16)

    u = jnp.concatenate(
        [jnp.concatenate([us_ref[b, cp] for b in range(nb)], axis=0) for cp in range(NC)], axis=1)

    y = jnp.dot(u, toep_ref[...], preferred_element_type=F32)
    x_end = jnp.dot(u, wb_ref[...], preferred_element_type=F32)

    jrow = lax.broadcasted_iota(jnp.int32, (M, 2 * P), 0) % nchunk
    lvl = lax.broadcasted_iota(jnp.int32, (SUBLANES, 2 * P), 0)
    nstep = (LS * jnp.left_shift(1, lvl)).astype(F32)
    el = jnp.exp(lam_re * dt * nstep)
    angl = lam_im * dt * nstep
    pl_all = el * jnp.cos(angl)
    ql_all = el * jnp.sin(angl) * sgn
    xs = x_end
    dstep = 1
    level = 0
    while dstep < nchunk:
        sh = jnp.where(jrow >= dstep, pltpu.roll(xs, dstep, 0), 0.0)
        xs = xs + pl_all[level:level + 1, :] * sh + ql_all[level:level + 1, :] * pltpu.roll(sh, P, 1)
        dstep *= 2
        level += 1
    x_prev = jnp.where(jrow >= 1, pltpu.roll(xs, 1, 0), 0.0)
    y = y + jnp.dot(x_prev.astype(BF16), wc_ref[...], preferred_element_type=F32)

    for c in range(NC):
        yc = y[:, c * LS:(c + 1) * LS]
        for b in range(nb):
            rows = slice(b * nchunk, (b + 1) * nchunk)
            yy = yc[rows, :] + d_ref[c:c + 1, 0:1] * us_ref[b, c].astype(F32)
            out_ref[b, c] = _gelu_tanh(yy).astype(BF16)


def _s5(ust4, ldt, lamr2, lami2, lamc, bt2, cc2, ca, cb, dcol):
    nb, w, nchunk, ls = ust4.shape
    G = w // S5_GROUP
    P2 = 2 * S5_STATE
    NC = S5_GROUP
    M = nb * nchunk
    g3 = lambda g: (g, 0, 0)
    g4 = lambda g: (g, 0, 0, 0)
    return pl.pallas_call(
        functools.partial(_s5_kernel, nb=nb, nchunk=nchunk),
        grid=(G,),
        in_specs=[pl.BlockSpec((nb, NC, nchunk, ls), lambda g: (0, g, 0, 0)),
                  pl.BlockSpec((None, 1, 1), g3),
                  pl.BlockSpec((None, 1, P2), g3),
                  pl.BlockSpec((None, 1, P2), g3),
                  pl.BlockSpec((None, 2, P2, 1), g4),
                  pl.BlockSpec((None, NC, P2), g3),
                  pl.BlockSpec((None, 2, NC, P2), g4),
                  pl.BlockSpec((None, P2, NC), g3),
                  pl.BlockSpec((None, P2, NC), g3),
                  pl.BlockSpec((None, NC, 1), g3)],
        out_specs=pl.BlockSpec((nb, NC, nchunk, ls), lambda g: (0, g, 0, 0)),
        out_shape=jax.ShapeDtypeStruct(ust4.shape, BF16),
        scratch_shapes=[pltpu.VMEM((NC * ls, NC * ls), BF16),
                        pltpu.VMEM((NC * ls, P2), BF16),
                        pltpu.VMEM((P2, NC * ls), BF16)],
        compiler_params=_params("arbitrary"),
        name="s5",
    )(ust4, ldt, lamr2, lami2, lamc, bt2, cc2, ca, cb, dcol)


def _tail_kernel(x_ref, mod_ref, ya_ref, sga_ref, yst_ref, sgbt_ref, wdown_ref, wglut_ref, wout_ref,
                 g1_ref, b1_ref, x1_ref, h2_ref, *, d, alpha):
    y_a = jnp.dot(ya_ref[...], wdown_ref[...], preferred_element_type=F32)
    vgt = jnp.dot(wglut_ref[...], yst_ref[...], preferred_element_type=F32)
    zbt = vgt[0:d, :] * _sigmoid(vgt[d:2 * d, :]) * sgbt_ref[...].astype(F32)
    z = sga_ref[...].astype(F32) * y_a + zbt.T
    mix = jnp.dot(z.astype(BF16), wout_ref[...], preferred_element_type=F32)
    r = alpha * x_ref[...] + (1.0 + mod_ref[2:3, :]) * mix
    x1 = _standardize(r) * g1_ref[...] + b1_ref[...]
    x1_ref[...] = x1
    h2_ref[...] = (_standardize(x1) * (1.0 + mod_ref[4:5, :]) + mod_ref[3:4, :]).astype(BF16)


def _tail(x, mod, ya, sga, yst, sgbt, wdown, wglut, wout, g1, b1, *, tm, alpha):
    bsz, s, d = x.shape
    s5w = yst.shape[1]
    tok = lambda b, i: (b, i, 0)
    chan = lambda b, i: (b, 0, i)
    const = lambda b, i: (0, 0)
    return pl.pallas_call(
        functools.partial(_tail_kernel, d=d, alpha=alpha),
        grid=(bsz, s // tm),
        in_specs=[pl.BlockSpec((None, tm, d), tok),
                  pl.BlockSpec((None, 6, d), lambda b, i: (b, 0, 0)),
                  pl.BlockSpec((None, tm, d), tok),
                  pl.BlockSpec((None, tm, d), tok),
                  pl.BlockSpec((None, s5w, tm), chan),
                  pl.BlockSpec((None, d, tm), chan),
                  pl.BlockSpec((d, d), const),
                  pl.BlockSpec((2 * d, s5w), const),
                  pl.BlockSpec((d, d), const),
                  pl.BlockSpec((1, d), const),
                  pl.BlockSpec((1, d), const)],
        out_specs=[pl.BlockSpec((None, tm, d), tok),
                   pl.BlockSpec((None, tm, d), tok)],
        out_shape=[jax.ShapeDtypeStruct((bsz, s, d), F32),
                   jax.ShapeDtypeStruct((bsz, s, d), BF16)],
        compiler_params=_params("arbitrary", "arbitrary"),
        name="tail",
    )(x, mod, ya, sga, yst, sgbt, wdown, wglut, wout, g1, b1)


def _ffn_kernel(x1_ref, h2_ref, mod_ref, wup_ref, wconv_ref, bconv_ref, wdown_ref, g2_ref, b2_ref,
                out_ref, gbuf, *, hidden, hchunk, alpha):
    tm = h2_ref.shape[0]

    @pl.when(pl.program_id(1) == 0)
    def _():
        gbuf[tm:tm + SUBLANES, :] = jnp.zeros((SUBLANES, hidden), F32)

    h2 = h2_ref[...]
    acc = jnp.zeros((tm, out_ref.shape[1]), F32)
    for c0 in range(0, hidden, hchunk):
        cols = slice(c0, c0 + hchunk)
        val = jnp.dot(h2, wup_ref[:, cols], preferred_element_type=F32)
        gate = jnp.dot(h2, wup_ref[:, hidden + c0:hidden + c0 + hchunk], preferred_element_type=F32)
        gbuf[0:SUBLANES, cols] = gbuf[tm:tm + SUBLANES, cols]
        gbuf[SUBLANES:SUBLANES + tm, cols] = gate
        conv = bconv_ref[:, cols] + wconv_ref[FFN_CONV - 1:FFN_CONV, cols] * gate
        for k in range(1, FFN_CONV):
            conv = conv + wconv_ref[FFN_CONV - 1 - k:FFN_CONV - k, cols] * gbuf[SUBLANES - k:SUBLANES - k + tm, cols]
        act = (_gelu_tanh(conv) * val).astype(BF16)
        acc = acc + jnp.dot(act, wdown_ref[cols, :], preferred_element_type=F32)
    r = alpha * x1_ref[...] + (1.0 + mod_ref[5:6, :]) * acc
    out_ref[...] = _standardize(r) * g2_ref[...] + b2_ref[...]


def _ffn(x1, h2, mod, wup, wconv, bconv, wdown, g2, b2, *, tm, hchunk, alpha):
    bsz, s, d = x1.shape
    hidden = wdown.shape[0]
    tok = lambda b, i: (b, i, 0)
    const = lambda b, i: (0, 0)
    return pl.pallas_call(
        functools.partial(_ffn_kernel, hidden=hidden, hchunk=hchunk, alpha=alpha),
        grid=(bsz, s // tm),
        in_specs=[pl.BlockSpec((None, tm, d), tok),
                  pl.BlockSpec((None, tm, d), tok),
                  pl.BlockSpec((None, 6, d), lambda b, i: (b, 0, 0)),
                  pl.BlockSpec((d, 2 * hidden), const),
                  pl.BlockSpec((FFN_CONV, hidden), const),
                  pl.BlockSpec((1, hidden), const),
                  pl.BlockSpec((hidden, d), const),
                  pl.BlockSpec((1, d), const),
                  pl.BlockSpec((1, d), const)],
        out_specs=pl.BlockSpec((None, tm, d), tok),
        out_shape=jax.ShapeDtypeStruct((bsz, s, d), F32),
        scratch_shapes=[pltpu.VMEM((tm + SUBLANES, hidden), F32)],
        compiler_params=_params("arbitrary", "arbitrary"),
        name="ffn",
    )(x1, h2, mod, wup, wconv, bconv, wdown, g2, b2)


def _layer(x, mod, w_in, b_in, w_mlstm_conv, b_mlstm_conv, w_mlstm_q, w_mlstm_k, mlstm_norm_gain,
           w_mlstm_down, s5_lam_re, s5_lam_im, s5_log_dt, s5_b_re, s5_b_im, s5_c_re, s5_c_im, s5_d,
           w_s5_glu, w_mix_out, ln1_gain, ln1_bias, w_ffn_up, w_ffn_conv, b_ffn_conv, w_ffn_down,
           ln2_gain, ln2_bias, *, alpha):
    bsz, s, d = x.shape
    H = MLSTM_HEADS
    s5w = s5_d.shape[0]
    G = s5w // S5_GROUP
    tm = min(512, s)
    chunk = min(256, s)

    o_om, o_ip, o_fp, o_us = d, 2 * d, 2 * d + H, 2 * d + 2 * H
    o_ga, o_gb = o_us + s5w, o_us + s5w + d
    tok_cols = jnp.concatenate([jnp.arange(0, 2 * d), jnp.arange(o_ga, o_ga + d)])
    chan_cols = jnp.concatenate([jnp.arange(o_us, o_us + s5w), jnp.arange(o_gb, o_gb + d),
                                 jnp.arange(o_ip, o_ip + 2 * H)])
    wtok = w_in[:, tok_cols].astype(BF16)
    btok = b_in[tok_cols].reshape(1, -1)
    wt = w_in[:, chan_cols].T.astype(BF16)
    bt = b_in[chan_cols].reshape(-1, 1)

    xm, som, sga, ust, sgbt, gt = _inproj(x, mod, wtok, btok, wt, bt, tm=tm, s5w=s5w)

    ya = _mlstm(xm, som, gt, w_mlstm_conv, b_mlstm_conv.reshape(1, d), w_mlstm_q.astype(BF16),
                w_mlstm_k.astype(BF16), mlstm_norm_gain.reshape(1, d), chunk=chunk)

    dup = lambda a: jnp.concatenate([a, a], axis=-1)
    lamr2 = dup(s5_lam_re)[:, None, :]
    lami2 = dup(s5_lam_im)[:, None, :]
    lamc = jnp.stack([dup(s5_lam_re), dup(s5_lam_im)], axis=1)[..., None]
    bt2 = jnp.concatenate([jnp.swapaxes(s5_b_re, 1, 2), jnp.swapaxes(s5_b_im, 1, 2)], axis=-1)
    cc2 = jnp.stack([dup(s5_c_re), dup(s5_c_im)], axis=1)
    crt = jnp.swapaxes(s5_c_re, 1, 2)
    cit = jnp.swapaxes(s5_c_im, 1, 2)
    ca = jnp.concatenate([crt, -cit], axis=1)
    cb = jnp.concatenate([-cit, -crt], axis=1)
    yst = _s5(ust.reshape(bsz, s5w, s // LANES, LANES), s5_log_dt.reshape(G, 1, 1), lamr2, lami2, lamc,
              bt2, cc2, ca, cb, s5_d.reshape(G, S5_GROUP, 1)).reshape(bsz, s5w, s)

    x1, h2 = _tail(x, mod, ya, sga, yst, sgbt, w_mlstm_down.astype(BF16), w_s5_glu.T.astype(BF16),
                   w_mix_out.astype(BF16), ln1_gain.reshape(1, d), ln1_bias.reshape(1, d), tm=tm, alpha=alpha)

    hidden = w_ffn_down.shape[0]
    return _ffn(x1, h2, mod, w_ffn_up.astype(BF16), w_ffn_conv, b_ffn_conv.reshape(1, hidden),
                w_ffn_down.astype(BF16), ln2_gain.reshape(1, d), ln2_bias.reshape(1, d),
                tm=tm, hchunk=256, alpha=alpha)


def kernel(x, c, w_ada, b_ada, w_in, b_in, w_mlstm_conv, b_mlstm_conv, w_mlstm_q, w_mlstm_k, mlstm_norm_gain, w_mlstm_down, s5_lam_re, s5_lam_im, s5_log_dt, s5_b_re, s5_b_im, s5_c_re, s5_c_im, s5_d, w_s5_glu, w_mix_out, ln1_gain, ln1_bias, w_ffn_up, w_ffn_conv, b_ffn_conv, w_ffn_down, ln2_gain, ln2_bias):
    depth = w_ada.shape[0]
    alpha = (2.0 * depth) ** 0.25
    bsz, d = c.shape
    for l in range(depth):
        mod = _adaln(c, w_ada[l], b_ada[l]).reshape(bsz, 6, d)
        x = _layer(x, mod, w_in[l], b_in[l], w_mlstm_conv[l], b_mlstm_conv[l], w_mlstm_q[l], w_mlstm_k[l],
                   mlstm_norm_gain[l], w_mlstm_down[l], s5_lam_re[l], s5_lam_im[l], s5_log_dt[l], s5_b_re[l],
                   s5_b_im[l], s5_c_re[l], s5_c_im[l], s5_d[l], w_s5_glu[l], w_mix_out[l], ln1_gain[l],
                   ln1_bias[l], w_ffn_up[l], w_ffn_conv[l], b_ffn_conv[l], w_ffn_down[l], ln2_gain[l],
                   ln2_bias[l], alpha=alpha)
    return x
```

```python
import functools
import math

import jax
import jax.numpy as jnp
from jax import lax
from jax.experimental import pallas as pl
from jax.experimental.pallas import tpu as pltpu

F32 = jnp.float32
BF16 = jnp.bfloat16
HIGHEST = lax.Precision.HIGHEST

LN_EPS = 1e-5
MLSTM_HEADS = 4
MLSTM_CONV = 4
FFN_CONV = 3
S5_GROUP = 16
S5_STATE = 64
S5_CBLOCK = 4
LANES = 128
SUBLANES = 8
VMEM_LIMIT_BYTES = 56 * 1024 * 1024

NT_DIMS = (((1,), (1,)), ((), ()))
TN_DIMS = (((0,), (0,)), ((), ()))


def _standardize(x):
    mu = jnp.mean(x, axis=-1, keepdims=True)
    xc = x - mu
    var = jnp.mean(xc * xc, axis=-1, keepdims=True)
    return xc * lax.rsqrt(var + LN_EPS)


def _sigmoid(x):
    return 1.0 / (1.0 + jnp.exp(-x))


def _gelu_tanh(x):
    return 0.5 * x * (1.0 + jnp.tanh(math.sqrt(2.0 / math.pi) * (x + 0.044715 * (x * x * x))))


def _params(*semantics):
    return pltpu.CompilerParams(dimension_semantics=semantics, vmem_limit_bytes=VMEM_LIMIT_BYTES)


def _adaln_kernel(c_ref, w_ref, b_ref, o_ref):
    c = c_ref[...]
    ca = c * _sigmoid(c)
    o_ref[...] = jnp.dot(ca, w_ref[...], precision=HIGHEST, preferred_element_type=F32) + b_ref[...]


def _adaln(c, w, b):
    bsz, d = c.shape
    n = w.shape[1]
    return pl.pallas_call(
        _adaln_kernel,
        grid=(n // d,),
        in_specs=[pl.BlockSpec((bsz, d), lambda j: (0, 0)),
                  pl.BlockSpec((d, d), lambda j: (0, j)),
                  pl.BlockSpec((1, d), lambda j: (0, j))],
        out_specs=pl.BlockSpec((bsz, d), lambda j: (0, j)),
        out_shape=jax.ShapeDtypeStruct((bsz, n), F32),
        compiler_params=_params("arbitrary"),
        name="adaln",
    )(c, w, b.reshape(1, n))


def _log_sigmoid(g):
    return jnp.minimum(g, 0.0) - jnp.log(1.0 + jnp.exp(-jnp.abs(g)))


def _inproj_kernel(x_ref, mod_ref, wtok_ref, btok_ref, wt_ref, bt_ref,
                   xm_ref, som_ref, sga_ref, ust_ref, sgbt_ref, gt_ref, *, d, s5w):
    x = x_ref[...]
    h = (_standardize(x) * (1.0 + mod_ref[1:2, :]) + mod_ref[0:1, :]).astype(BF16)
    p = jnp.dot(h, wtok_ref[:, 0:d], preferred_element_type=F32) + btok_ref[:, 0:d]
    xm_ref[...] = p.astype(BF16)
    p = jnp.dot(h, wtok_ref[:, d:2 * d], preferred_element_type=F32) + btok_ref[:, d:2 * d]
    som_ref[...] = _sigmoid(p).astype(BF16)
    p = jnp.dot(h, wtok_ref[:, 2 * d:3 * d], preferred_element_type=F32) + btok_ref[:, 2 * d:3 * d]
    sga_ref[...] = _sigmoid(p).astype(BF16)
    pt = lax.dot_general(wt_ref[0:s5w, :], h, NT_DIMS, preferred_element_type=F32) + bt_ref[0:s5w, :]
    ust_ref[...] = pt.astype(BF16)
    pt = lax.dot_general(wt_ref[s5w:s5w + d, :], h, NT_DIMS, preferred_element_type=F32) + bt_ref[s5w:s5w + d, :]
    sgbt_ref[...] = _sigmoid(pt).astype(BF16)
    ng = 2 * MLSTM_HEADS
    pt = lax.dot_general(wt_ref[s5w + d:s5w + d + ng, :], h, NT_DIMS, preferred_element_type=F32)
    gt_ref[...] = pt + bt_ref[s5w + d:s5w + d + ng, :]


def _inproj(x, mod, wtok, btok, wt, bt, *, tm, s5w):
    bsz, s, d = x.shape
    nt = wt.shape[0]
    ng = 2 * MLSTM_HEADS
    tok = lambda b, i: (b, i, 0)
    chan = lambda b, i: (b, 0, i)
    const = lambda b, i: (0, 0)
    return pl.pallas_call(
        functools.partial(_inproj_kernel, d=d, s5w=s5w),
        grid=(bsz, s // tm),
        in_specs=[pl.BlockSpec((None, tm, d), tok),
                  pl.BlockSpec((None, 6, d), lambda b, i: (b, 0, 0)),
                  pl.BlockSpec((d, 3 * d), const),
                  pl.BlockSpec((1, 3 * d), const),
                  pl.BlockSpec((nt, d), const),
                  pl.BlockSpec((nt, 1), const)],
        out_specs=[pl.BlockSpec((None, tm, d), tok),
                   pl.BlockSpec((None, tm, d), tok),
                   pl.BlockSpec((None, tm, d), tok),
                   pl.BlockSpec((None, s5w, tm), chan),
                   pl.BlockSpec((None, d, tm), chan),
                   pl.BlockSpec((None, ng, tm), chan)],
        out_shape=[jax.ShapeDtypeStruct((bsz, s, d), BF16),
                   jax.ShapeDtypeStruct((bsz, s, d), BF16),
                   jax.ShapeDtypeStruct((bsz, s, d), BF16),
                   jax.ShapeDtypeStruct((bsz, s5w, s), BF16),
                   jax.ShapeDtypeStruct((bsz, d, s), BF16),
                   jax.ShapeDtypeStruct((bsz, ng, s), F32)],
        compiler_params=_params("arbitrary", "arbitrary"),
        name="inproj",
    )(x, mod, wtok, btok, wt, bt)


def _lane_cumsum(x):
    n = x.shape[-1]
    lane = lax.broadcasted_iota(jnp.int32, x.shape, x.ndim - 1)
    sh = 1
    while sh < n:
        x = x + jnp.where(lane >= sh, pltpu.roll(x, sh, x.ndim - 1), 0.0)
        sh *= 2
    return x


def _mlstm_kernel(xm_ref, som_ref, gt_ref, wconv_ref, bconv_ref, wq_ref, wk_ref, gain_ref,
                  out_ref, xbuf, c_ref, n_ref, m_ref, *, chunk, dv, dk):
    L = chunk
    H = MLSTM_HEADS

    @pl.when(pl.program_id(1) == 0)
    def _():
        xbuf[L:L + SUBLANES, :] = jnp.zeros((SUBLANES, xbuf.shape[1]), F32)
        c_ref[...] = jnp.zeros(c_ref.shape, F32)
        n_ref[...] = jnp.zeros(n_ref.shape, F32)
        m_ref[...] = jnp.zeros(m_ref.shape, F32)

    xbuf[0:SUBLANES, :] = xbuf[L:L + SUBLANES, :]
    xbuf[SUBLANES:SUBLANES + L, :] = xm_ref[...].astype(F32)

    g = gt_ref[...]
    row = lax.broadcasted_iota(jnp.int32, g.shape, 0)
    bcum = _lane_cumsum(jnp.where(row >= H, _log_sigmoid(g), 0.0))
    r_all = jnp.where(row >= H, bcum, g)
    eye = (lax.broadcasted_iota(jnp.int32, (L, L), 0) == lax.broadcasted_iota(jnp.int32, (L, L), 1)).astype(F32)
    c_all = lax.dot_general(eye, r_all, NT_DIMS, precision=HIGHEST, preferred_element_type=F32)

    ti = lax.broadcasted_iota(jnp.int32, (L, L), 0)
    si = lax.broadcasted_iota(jnp.int32, (L, L), 1)
    causal = si <= ti
    scale = dk ** -0.5

    for h in range(H):
        cols = slice(h * dv, (h + 1) * dv)
        ig_s = r_all[h:h + 1, :]
        b_s = r_all[H + h:H + h + 1, :]
        ig_t = c_all[:, h:h + 1]
        b_t = c_all[:, H + h:H + h + 1]
        b_last = b_s[:, L - 1:L]
        m_prev = m_ref[h, 0:1, 0:1]

        acc = bconv_ref[:, cols]
        for j in range(MLSTM_CONV):
            off = SUBLANES - (MLSTM_CONV - 1) + j
            acc = acc + wconv_ref[j:j + 1, cols] * xbuf[off:off + L, cols]
        xc = (acc * _sigmoid(acc)).astype(BF16)
        q = jnp.dot(xc, wq_ref[h], preferred_element_type=F32) * scale
        k = jnp.dot(xc, wk_ref[h], preferred_element_type=F32)
        qb = q.astype(BF16)
        kb = k.astype(BF16)
        v = xm_ref[:, cols]

        dmat = jnp.where(causal, b_t - b_s + ig_s, -jnp.inf)
        inter = b_t + m_prev
        m_t = jnp.maximum(inter, jnp.max(dmat, axis=1, keepdims=True))
        wmat = jnp.exp(dmat - m_t)
        sc_inter = jnp.exp(inter - m_t)
        s = lax.dot_general(qb, kb, NT_DIMS, preferred_element_type=F32) * wmat
        cmat = c_ref[h]
        num = (jnp.dot(s.astype(BF16), v, preferred_element_type=F32)
               + sc_inter * jnp.dot(qb, cmat.astype(BF16), preferred_element_type=F32))
        den = (jnp.sum(s, axis=1, keepdims=True)
               + sc_inter * jnp.sum(q * n_ref[h, 0:1, :], axis=1, keepdims=True))
        hh = num / jnp.maximum(jnp.abs(den), jnp.exp(-m_t))
        hm = _standardize(hh)
        out_ref[:, cols] = (hm * gain_ref[:, cols] * som_ref[:, cols].astype(F32)).astype(BF16)

        g_s = b_last - b_s + ig_s
        m_new = jnp.maximum(b_last + m_prev, jnp.max(g_s, axis=1, keepdims=True))
        wk_t = jnp.exp(b_last - b_t + ig_t - m_new)
        decay = jnp.exp(b_last + m_prev - m_new)
        kw = k * wk_t
        c_ref[h] = decay * cmat + lax.dot_general(kw.astype(BF16), v, TN_DIMS, preferred_element_type=F32)
        n_ref[h, 0:1, :] = decay * n_ref[h, 0:1, :] + jnp.sum(kw, axis=0, keepdims=True)
        m_ref[h] = jnp.broadcast_to(m_new, m_ref.shape[1:])


def _mlstm(xm, som, gt, wconv, bconv, wq, wk, gain, *, chunk):
    bsz, s, d = xm.shape
    H = MLSTM_HEADS
    dv = d // H
    dk = wq.shape[-1]
    ng = gt.shape[1]
    tok = lambda b, j: (b, j, 0)
    const2 = lambda b, j: (0, 0)
    const3 = lambda b, j: (0, 0, 0)
    return pl.pallas_call(
        functools.partial(_mlstm_kernel, chunk=chunk, dv=dv, dk=dk),
        grid=(bsz, s // chunk),
        in_specs=[pl.BlockSpec((None, chunk, d), tok),
                  pl.BlockSpec((None, chunk, d), tok),
                  pl.BlockSpec((None, ng, chunk), lambda b, j: (b, 0, j)),
                  pl.BlockSpec((MLSTM_CONV, d), const2),
                  pl.BlockSpec((1, d), const2),
                  pl.BlockSpec((H, dv, dk), const3),
                  pl.BlockSpec((H, dv, dk), const3),
                  pl.BlockSpec((1, d), const2)],
        out_specs=pl.BlockSpec((None, chunk, d), tok),
        out_shape=jax.ShapeDtypeStruct((bsz, s, d), BF16),
        scratch_shapes=[pltpu.VMEM((chunk + SUBLANES, d), F32),
                        pltpu.VMEM((H, dk, dv), F32),
                        pltpu.VMEM((H, SUBLANES, dk), F32),
                        pltpu.VMEM((H, SUBLANES, LANES), F32)],
        compiler_params=_params("arbitrary", "arbitrary"),
        name="mlstm",
    )(xm, som, gt, wconv, bconv, wq, wk, gain)


def _s5_kernel(us_ref, ldt_ref, lamr_ref, lami_ref, lamc_ref, bt_ref, cc_ref, ca_ref, cb_ref, d_ref,
               out_ref, toep_ref, wb_ref, wc_ref, *, nb, nchunk):
    P = S5_STATE
    NC = S5_GROUP
    LS = LANES
    M = nb * nchunk

    dt = jnp.exp(ldt_ref[...])
    lam_re = lamr_ref[...]
    lam_im = lami_ref[...]
    lane2 = lax.broadcasted_iota(jnp.int32, (1, 2 * P), 1)
    sgn = jnp.where(lane2 < P, -1.0, 1.0)

    mag = jnp.exp(lam_re * dt)
    ar = mag * jnp.cos(lam_im * dt)
    ai = mag * jnp.sin(lam_im * dt)
    den = lam_re * lam_re + lam_im * lam_im
    zr = ((ar - 1.0) * lam_re + ai * lam_im) / den
    zi = (ai * lam_re - (ar - 1.0) * lam_im) / den
    b1 = bt_ref[...]
    bb = zr * b1 + zi * sgn * pltpu.roll(b1, P, 1)
    bb_sw = pltpu.roll(bb, P, 1)

    cr2 = cc_ref[0]
    ci2 = cc_ref[1]
    m12 = (cr2[:, None, :] * (bb * (-sgn))[None, :, :]
           - ci2[:, None, :] * bb_sw[None, :, :]).reshape(NC * NC, 2 * P)
    lam_re_c = lamc_ref[0]
    lam_im_c = lamc_ref[1]
    tau = lax.broadcasted_iota(jnp.int32, (2 * P, LS), 1).astype(F32)
    prow = lax.broadcasted_iota(jnp.int32, (2 * P, LS), 0)
    e0 = jnp.exp(lam_re_c * dt * tau)
    ang0 = lam_im_c * dt * tau
    pr0 = e0 * jnp.cos(ang0)
    pi0 = e0 * jnp.sin(ang0)
    pstack = jnp.where(prow < P, pr0, pi0)
    kpairs = jnp.dot(m12, pstack, precision=HIGHEST, preferred_element_type=F32)

    srow = lax.broadcasted_iota(jnp.int32, (LS, LS), 0)
    tcol = lax.broadcasted_iota(jnp.int32, (LS, LS), 1)
    lower = tcol >= srow
    srev = (LS - 1) - lax.broadcasted_iota(jnp.int32, (LS, 2 * P), 0).astype(F32)
    e1 = jnp.exp(lam_re * dt * srev)
    ang1 = lam_im * dt * srev
    r2 = e1 * jnp.cos(ang1)
    i2 = e1 * jnp.sin(ang1)
    bb_rot = bb_sw * sgn

    y = None
    x_end = None
    for cp0 in range(0, NC, S5_CBLOCK):
        for cp in range(cp0, cp0 + S5_CBLOCK):
            for c in range(NC):
                r = c * NC + cp
                kv = jnp.broadcast_to(kpairs[r:r + 1, :], (LS, LS))
                tz = pltpu.roll(kv, 0, 1, stride=1, stride_axis=0)
                toep_ref[cp * LS:(cp + 1) * LS, c * LS:(c + 1) * LS] = jnp.where(lower, tz, 0.0).astype(BF16)
            wb_ref[cp * LS:(cp + 1) * LS, :] = (r2 * bb[cp:cp + 1, :] + i2 * bb_rot[cp:cp + 1, :]).astype(BF16)
        rows = slice(cp0 * LS, (cp0 + S5_CBLOCK) * LS)
        ub = jnp.concatenate(
            [jnp.concatenate([us_ref[b, cp] for b in range(nb)], axis=0) for cp in range(cp0, cp0 + S5_CBLOCK)],
            axis=1)
        yp = jnp.dot(ub, toep_ref[rows, :], preferred_element_type=F32)
        xp = jnp.dot(ub, wb_ref[rows, :], preferred_element_type=F32)
        y = yp if y is None else y + yp
        x_end = xp if x_end is None else x_end + xp

    ar_c = jnp.exp(lam_re_c * dt) * jnp.cos(lam_im_c * dt)
    ai_c = jnp.exp(lam_re_c * dt) * jnp.sin(lam_im_c * dt)
    pr1 = pr0 * ar_c - pi0 * ai_c
    pi1 = pr0 * ai_c + pi0 * ar_c
    ca = ca_ref[...]
    cb = cb_ref[...]
    for c in range(NC):
        wc_ref[:, c * LS:(c + 1) * LS] = (ca[:, c:c + 1] * pr1 + cb[:, c:c + 1] * pi1).astype(BF16)

    jrow = lax.broadcasted_iota(jnp.int32, (M, 2 * P), 0) % nchunk
    lvl = lax.broadcasted_iota(jnp.int32, (SUBLANES, 2 * P), 0)
    nstep = (LS * jnp.left_shift(1, lvl)).astype(F32)
    el = jnp.exp(lam_re * dt * nstep)
    angl = lam_im * dt * nstep
    pl_all = el * jnp.cos(angl)
    ql_all = el * jnp.sin(angl) * sgn
    xs = x_end
    dstep = 1
    level = 0
    while dstep < nchunk:
        sh = jnp.where(jrow >= dstep, pltpu.roll(xs, dstep, 0), 0.0)
        xs = xs + pl_all[level:level + 1, :] * sh + ql_all[level:level + 1, :] * pltpu.roll(sh, P, 1)
        dstep *= 2
        level += 1
    x_prev = jnp.where(jrow >= 1, pltpu.roll(xs, 1, 0), 0.0)
    y = y + jnp.dot(x_prev.astype(BF16), wc_ref[...], preferred_element_type=F32)

    for c in range(NC):
        yc = y[:, c * LS:(c + 1) * LS]
        for b in range(nb):
            rows = slice(b * nchunk, (b + 1) * nchunk)
            yy = yc[rows, :] + d_ref[c:c + 1, 0:1] * us_ref[b, c].astype(F32)
            out_ref[b, c] = _gelu_tanh(yy).astype(BF16)


def _s5(ust4, ldt, lamr2, lami2, lamc, bt2, cc2, ca, cb, dcol):
    nb, w, nchunk, ls = ust4.shape
    G = w // S5_GROUP
    P2 = 2 * S5_STATE
    NC = S5_GROUP
    M = nb * nchunk
    g3 = lambda g: (g, 0, 0)
    g4 = lambda g: (g, 0, 0, 0)
    return pl.pallas_call(
        functools.partial(_s5_kernel, nb=nb, nchunk=nchunk),
        grid=(G,),
        in_specs=[pl.BlockSpec((nb, NC, nchunk, ls), lambda g: (0, g, 0, 0)),
                  pl.BlockSpec((None, 1, 1), g3),
                  pl.BlockSpec((None, 1, P2), g3),
                  pl.BlockSpec((None, 1, P2), g3),
                  pl.BlockSpec((None, 2, P2, 1), g4),
                  pl.BlockSpec((None, NC, P2), g3),
                  pl.BlockSpec((None, 2, NC, P2), g4),
                  pl.BlockSpec((None, P2, NC), g3),
                  pl.BlockSpec((None, P2, NC), g3),
                  pl.BlockSpec((None, NC, 1), g3)],
        out_specs=pl.BlockSpec((nb, NC, nchunk, ls), lambda g: (0, g, 0, 0)),
        out_shape=jax.ShapeDtypeStruct(ust4.shape, BF16),
        scratch_shapes=[pltpu.VMEM((NC * ls, NC * ls), BF16),
                        pltpu.VMEM((NC * ls, P2), BF16),
                        pltpu.VMEM((P2, NC * ls), BF16)],
        compiler_params=_params("arbitrary"),
        name="s5",
    )(ust4, ldt, lamr2, lami2, lamc, bt2, cc2, ca, cb, dcol)


def _tail_kernel(x_ref, mod_ref, ya_ref, sga_ref, yst_ref, sgbt_ref, wdown_ref, wglut_ref, wout_ref,
                 g1_ref, b1_ref, x1_ref, h2_ref, *, d, alpha):
    y_a = jnp.dot(ya_ref[...], wdown_ref[...], preferred_element_type=F32)
    vgt = jnp.dot(wglut_ref[...], yst_ref[...], preferred_element_type=F32)
    zbt = vgt[0:d, :] * _sigmoid(vgt[d:2 * d, :]) * sgbt_ref[...].astype(F32)
    z = sga_ref[...].astype(F32) * y_a + zbt.T
    mix = jnp.dot(z.astype(BF16), wout_ref[...], preferred_element_type=F32)
    r = alpha * x_ref[...] + (1.0 + mod_ref[2:3, :]) * mix
    x1 = _standardize(r) * g1_ref[...] + b1_ref[...]
    x1_ref[...] = x1
    h2_ref[...] = (_standardize(x1) * (1.0 + mod_ref[4:5, :]) + mod_ref[3:4, :]).astype(BF16)


def _tail(x, mod, ya, sga, yst, sgbt, wdown, wglut, wout, g1, b1, *, tm, alpha):
    bsz, s, d = x.shape
    s5w = yst.shape[1]
    tok = lambda b, i: (b, i, 0)
    chan = lambda b, i: (b, 0, i)
    const = lambda b, i: (0, 0)
    return pl.pallas_call(
        functools.partial(_tail_kernel, d=d, alpha=alpha),
        grid=(bsz, s // tm),
        in_specs=[pl.BlockSpec((None, tm, d), tok),
                  pl.BlockSpec((None, 6, d), lambda b, i: (b, 0, 0)),
                  pl.BlockSpec((None, tm, d), tok),
                  pl.BlockSpec((None, tm, d), tok),
                  pl.BlockSpec((None, s5w, tm), chan),
                  pl.BlockSpec((None, d, tm), chan),
                  pl.BlockSpec((d, d), const),
                  pl.BlockSpec((2 * d, s5w), const),
                  pl.BlockSpec((d, d), const),
                  pl.BlockSpec((1, d), const),
                  pl.BlockSpec((1, d), const)],
        out_specs=[pl.BlockSpec((None, tm, d), tok),
                   pl.BlockSpec((None, tm, d), tok)],
        out_shape=[jax.ShapeDtypeStruct((bsz, s, d), F32),
                   jax.ShapeDtypeStruct((bsz, s, d), BF16)],
        compiler_params=_params("arbitrary", "arbitrary"),
        name="tail",
    )(x, mod, ya, sga, yst, sgbt, wdown, wglut, wout, g1, b1)


def _ffn_kernel(x1_ref, h2_ref, mod_ref, wup_ref, wconv_ref, bconv_ref, wdown_ref, g2_ref, b2_ref,
                out_ref, gbuf, *, hidden, hchunk, alpha):
    tm = h2_ref.shape[0]

    @pl.when(pl.program_id(1) == 0)
    def _():
        gbuf[tm:tm + SUBLANES, :] = jnp.zeros((SUBLANES, hidden), F32)

    h2 = h2_ref[...]
    acc = jnp.zeros((tm, out_ref.shape[1]), F32)
    for c0 in range(0, hidden, hchunk):
        cols = slice(c0, c0 + hchunk)
        val = jnp.dot(h2, wup_ref[:, cols], preferred_element_type=F32)
        gate = jnp.dot(h2, wup_ref[:, hidden + c0:hidden + c0 + hchunk], preferred_element_type=F32)
        gbuf[0:SUBLANES, cols] = gbuf[tm:tm + SUBLANES, cols]
        gbuf[SUBLANES:SUBLANES + tm, cols] = gate
        conv = bconv_ref[:, cols] + wconv_ref[FFN_CONV - 1:FFN_CONV, cols] * gate
        for k in range(1, FFN_CONV):
            conv = conv + wconv_ref[FFN_CONV - 1 - k:FFN_CONV - k, cols] * gbuf[SUBLANES - k:SUBLANES - k + tm, cols]
        act = (_gelu_tanh(conv) * val).astype(BF16)
        acc = acc + jnp.dot(act, wdown_ref[cols, :], preferred_element_type=F32)
    r = alpha * x1_ref[...] + (1.0 + mod_ref[5:6, :]) * acc
    out_ref[...] = _standardize(r) * g2_ref[...] + b2_ref[...]


def _ffn(x1, h2, mod, wup, wconv, bconv, wdown, g2, b2, *, tm, hchunk, alpha):
    bsz, s, d = x1.shape
    hidden = wdown.shape[0]
    tok = lambda b, i: (b, i, 0)
    const = lambda b, i: (0, 0)
    return pl.pallas_call(
        functools.partial(_ffn_kernel, hidden=hidden, hchunk=hchunk, alpha=alpha),
        grid=(bsz, s // tm),
        in_specs=[pl.BlockSpec((None, tm, d), tok),
                  pl.BlockSpec((None, tm, d), tok),
                  pl.BlockSpec((None, 6, d), lambda b, i: (b, 0, 0)),
                  pl.BlockSpec((d, 2 * hidden), const),
                  pl.BlockSpec((FFN_CONV, hidden), const),
                  pl.BlockSpec((1, hidden), const),
                  pl.BlockSpec((hidden, d), const),
                  pl.BlockSpec((1, d), const),
                  pl.BlockSpec((1, d), const)],
        out_specs=pl.BlockSpec((None, tm, d), tok),
        out_shape=jax.ShapeDtypeStruct((bsz, s, d), F32),
        scratch_shapes=[pltpu.VMEM((tm + SUBLANES, hidden), F32)],
        compiler_params=_params("arbitrary", "arbitrary"),
        name="ffn",
    )(x1, h2, mod, wup, wconv, bconv, wdown, g2, b2)


def _layer(x, mod, w_in, b_in, w_mlstm_conv, b_mlstm_conv, w_mlstm_q, w_mlstm_k, mlstm_norm_gain,
           w_mlstm_down, s5_lam_re, s5_lam_im, s5_log_dt, s5_b_re, s5_b_im, s5_c_re, s5_c_im, s5_d,
           w_s5_glu, w_mix_out, ln1_gain, ln1_bias, w_ffn_up, w_ffn_conv, b_ffn_conv, w_ffn_down,
           ln2_gain, ln2_bias, *, alpha):
    bsz, s, d = x.shape
    H = MLSTM_HEADS
    s5w = s5_d.shape[0]
    G = s5w // S5_GROUP
    tm = min(512, s)
    chunk = min(256, s)

    o_om, o_ip, o_fp, o_us = d, 2 * d, 2 * d + H, 2 * d + 2 * H
    o_ga, o_gb = o_us + s5w, o_us + s5w + d
    tok_cols = jnp.concatenate([jnp.arange(0, 2 * d), jnp.arange(o_ga, o_ga + d)])
    chan_cols = jnp.concatenate([jnp.arange(o_us, o_us + s5w), jnp.arange(o_gb, o_gb + d),
                                 jnp.arange(o_ip, o_ip + 2 * H)])
    wtok = w_in[:, tok_cols].astype(BF16)
    btok = b_in[tok_cols].reshape(1, -1)
    wt = w_in[:, chan_cols].T.astype(BF16)
    bt = b_in[chan_cols].reshape(-1, 1)

    xm, som, sga, ust, sgbt, gt = _inproj(x, mod, wtok, btok, wt, bt, tm=tm, s5w=s5w)

    ya = _mlstm(xm, som, gt, w_mlstm_conv, b_mlstm_conv.reshape(1, d), w_mlstm_q.astype(BF16),
                w_mlstm_k.astype(BF16), mlstm_norm_gain.reshape(1, d), chunk=chunk)

    dup = lambda a: jnp.concatenate([a, a], axis=-1)
    lamr2 = dup(s5_lam_re)[:, None, :]
    lami2 = dup(s5_lam_im)[:, None, :]
    lamc = jnp.stack([dup(s5_lam_re), dup(s5_lam_im)], axis=1)[..., None]
    bt2 = jnp.concatenate([jnp.swapaxes(s5_b_re, 1, 2), jnp.swapaxes(s5_b_im, 1, 2)], axis=-1)
    cc2 = jnp.stack([dup(s5_c_re), dup(s5_c_im)], axis=1)
    crt = jnp.swapaxes(s5_c_re, 1, 2)
    cit = jnp.swapaxes(s5_c_im, 1, 2)
    ca = jnp.concatenate([crt, -cit], axis=1)
    cb = jnp.concatenate([-cit, -crt], axis=1)
    yst = _s5(ust.reshape(bsz, s5w, s // LANES, LANES), s5_log_dt.reshape(G, 1, 1), lamr2, lami2, lamc,
              bt2, cc2, ca, cb, s5_d.reshape(G, S5_GROUP, 1)).reshape(bsz, s5w, s)

    x1, h2 = _tail(x, mod, ya, sga, yst, sgbt, w_mlstm_down.astype(BF16), w_s5_glu.T.astype(BF16),
                   w_mix_out.astype(BF16), ln1_gain.reshape(1, d), ln1_bias.reshape(1, d), tm=tm, alpha=alpha)

    hidden = w_ffn_down.shape[0]
    return _ffn(x1, h2, mod, w_ffn_up.astype(BF16), w_ffn_conv, b_ffn_conv.reshape(1, hidden),
                w_ffn_down.astype(BF16), ln2_gain.reshape(1, d), ln2_bias.reshape(1, d),
                tm=tm, hchunk=hidden, alpha=alpha)


def kernel(x, c, w_ada, b_ada, w_in, b_in, w_mlstm_conv, b_mlstm_conv, w_mlstm_q, w_mlstm_k, mlstm_norm_gain, w_mlstm_down, s5_lam_re, s5_lam_im, s5_log_dt, s5_b_re, s5_b_im, s5_c_re, s5_c_im, s5_d, w_s5_glu, w_mix_out, ln1_gain, ln1_bias, w_ffn_up, w_ffn_conv, b_ffn_conv, w_ffn_down, ln2_gain, ln2_bias):
    depth = w_ada.shape[0]
    alpha = (2.0 * depth) ** 0.25
    bsz, d = c.shape
    for l in range(depth):
        mod = _adaln(c, w_ada[l], b_ada[l]).reshape(bsz, 6, d)
        x = _layer(x, mod, w_in[l], b_in[l], w_mlstm_conv[l], b_mlstm_conv[l], w_mlstm_q[l], w_mlstm_k[l],
                   mlstm_norm_gain[l], w_mlstm_down[l], s5_lam_re[l], s5_lam_im[l], s5_log_dt[l], s5_b_re[l],
                   s5_b_im[l], s5_c_re[l], s5_c_im[l], s5_d[l], w_s5_glu[l], w_mix_out[l], ln1_gain[l],
                   ln1_bias[l], w_ffn_up[l], w_ffn_conv[l], b_ffn_conv[l], w_ffn_down[l], ln2_gain[l],
                   ln2_bias[l], alpha=alpha)
    return x
```

```python
import functools
import math

import jax
import jax.numpy as jnp
from jax import lax
from jax.experimental import pallas as pl
from jax.experimental.pallas import tpu as pltpu

F32 = jnp.float32
BF16 = jnp.bfloat16
HIGHEST = lax.Precision.HIGHEST

LN_EPS = 1e-5
MLSTM_HEADS = 4
MLSTM_CONV = 4
FFN_CONV = 3
S5_GROUP = 16
S5_STATE = 64
S5_CBLOCK = 4
LANES = 128
SUBLANES = 8
VMEM_LIMIT_BYTES = 56 * 1024 * 1024

NT_DIMS = (((1,), (1,)), ((), ()))
TN_DIMS = (((0,), (0,)), ((), ()))


def _standardize(x):
    mu = jnp.mean(x, axis=-1, keepdims=True)
    xc = x - mu
    var = jnp.mean(xc * xc, axis=-1, keepdims=True)
    return xc * lax.rsqrt(var + LN_EPS)


def _sigmoid(x):
    return 1.0 / (1.0 + jnp.exp(-x))


def _gelu_tanh(x):
    return 0.5 * x * (1.0 + jnp.tanh(math.sqrt(2.0 / math.pi) * (x + 0.044715 * (x * x * x))))


def _params(*semantics):
    return pltpu.CompilerParams(dimension_semantics=semantics, vmem_limit_bytes=VMEM_LIMIT_BYTES)


def _adaln_kernel(c_ref, w_ref, b_ref, o_ref):
    c = c_ref[...]
    ca = c * _sigmoid(c)
    o_ref[...] = jnp.dot(ca, w_ref[...], precision=HIGHEST, preferred_element_type=F32) + b_ref[...]


def _adaln(c, w, b):
    bsz, d = c.shape
    n = w.shape[1]
    return pl.pallas_call(
        _adaln_kernel,
        grid=(n // d,),
        in_specs=[pl.BlockSpec((bsz, d), lambda j: (0, 0)),
                  pl.BlockSpec((d, d), lambda j: (0, j)),
                  pl.BlockSpec((1, d), lambda j: (0, j))],
        out_specs=pl.BlockSpec((bsz, d), lambda j: (0, j)),
        out_shape=jax.ShapeDtypeStruct((bsz, n), F32),
        compiler_params=_params("arbitrary"),
        name="adaln",
    )(c, w, b.reshape(1, n))


def _log_sigmoid(g):
    return jnp.minimum(g, 0.0) - jnp.log(1.0 + jnp.exp(-jnp.abs(g)))


def _inproj_kernel(x_ref, mod_ref, wtok_ref, btok_ref, wt_ref, bt_ref,
                   xm_ref, som_ref, sga_ref, ust_ref, sgbt_ref, gt_ref, *, d, s5w):
    x = x_ref[...]
    h = (_standardize(x) * (1.0 + mod_ref[1:2, :]) + mod_ref[0:1, :]).astype(BF16)
    p = jnp.dot(h, wtok_ref[:, 0:d], preferred_element_type=F32) + btok_ref[:, 0:d]
    xm_ref[...] = p.astype(BF16)
    p = jnp.dot(h, wtok_ref[:, d:2 * d], preferred_element_type=F32) + btok_ref[:, d:2 * d]
    som_ref[...] = _sigmoid(p).astype(BF16)
    p = jnp.dot(h, wtok_ref[:, 2 * d:3 * d], preferred_element_type=F32) + btok_ref[:, 2 * d:3 * d]
    sga_ref[...] = _sigmoid(p).astype(BF16)
    pt = lax.dot_general(wt_ref[0:s5w, :], h, NT_DIMS, preferred_element_type=F32) + bt_ref[0:s5w, :]
    ust_ref[...] = pt.astype(BF16)
    pt = lax.dot_general(wt_ref[s5w:s5w + d, :], h, NT_DIMS, preferred_element_type=F32) + bt_ref[s5w:s5w + d, :]
    sgbt_ref[...] = _sigmoid(pt).astype(BF16)
    ng = 2 * MLSTM_HEADS
    pt = lax.dot_general(wt_ref[s5w + d:s5w + d + ng, :], h, NT_DIMS, preferred_element_type=F32)
    gt_ref[...] = pt + bt_ref[s5w + d:s5w + d + ng, :]


def _inproj(x, mod, wtok, btok, wt, bt, *, tm, s5w):
    bsz, s, d = x.shape
    nt = wt.shape[0]
    ng = 2 * MLSTM_HEADS
    tok = lambda b, i: (b, i, 0)
    chan = lambda b, i: (b, 0, i)
    const = lambda b, i: (0, 0)
    return pl.pallas_call(
        functools.partial(_inproj_kernel, d=d, s5w=s5w),
        grid=(bsz, s // tm),
        in_specs=[pl.BlockSpec((None, tm, d), tok),
                  pl.BlockSpec((None, 6, d), lambda b, i: (b, 0, 0)),
                  pl.BlockSpec((d, 3 * d), const),
                  pl.BlockSpec((1, 3 * d), const),
                  pl.BlockSpec((nt, d), const),
                  pl.BlockSpec((nt, 1), const)],
        out_specs=[pl.BlockSpec((None, tm, d), tok),
                   pl.BlockSpec((None, tm, d), tok),
                   pl.BlockSpec((None, tm, d), tok),
                   pl.BlockSpec((None, s5w, tm), chan),
                   pl.BlockSpec((None, d, tm), chan),
                   pl.BlockSpec((None, ng, tm), chan)],
        out_shape=[jax.ShapeDtypeStruct((bsz, s, d), BF16),
                   jax.ShapeDtypeStruct((bsz, s, d), BF16),
                   jax.ShapeDtypeStruct((bsz, s, d), BF16),
                   jax.ShapeDtypeStruct((bsz, s5w, s), BF16),
                   jax.ShapeDtypeStruct((bsz, d, s), BF16),
                   jax.ShapeDtypeStruct((bsz, ng, s), F32)],
        compiler_params=_params("arbitrary", "arbitrary"),
        name="inproj",
    )(x, mod, wtok, btok, wt, bt)


def _lane_cumsum(x):
    n = x.shape[-1]
    lane = lax.broadcasted_iota(jnp.int32, x.shape, x.ndim - 1)
    sh = 1
    while sh < n:
        x = x + jnp.where(lane >= sh, pltpu.roll(x, sh, x.ndim - 1), 0.0)
        sh *= 2
    return x


def _mlstm_kernel(xm_ref, gt_ref, wconv_ref, bconv_ref, wq_ref, wk_ref,
                  out_ref, xbuf, c_ref, n_ref, m_ref, *, chunk, dv, dk):
    L = chunk
    H = MLSTM_HEADS

    @pl.when(pl.program_id(1) == 0)
    def _():
        xbuf[L:L + SUBLANES, :] = jnp.zeros((SUBLANES, xbuf.shape[1]), F32)
        c_ref[...] = jnp.zeros(c_ref.shape, F32)
        n_ref[...] = jnp.zeros(n_ref.shape, F32)
        m_ref[...] = jnp.zeros(m_ref.shape, F32)

    xbuf[0:SUBLANES, :] = xbuf[L:L + SUBLANES, :]
    xbuf[SUBLANES:SUBLANES + L, :] = xm_ref[...].astype(F32)

    g = gt_ref[...]
    row = lax.broadcasted_iota(jnp.int32, g.shape, 0)
    bcum = _lane_cumsum(jnp.where(row >= H, _log_sigmoid(g), 0.0))
    r_all = jnp.where(row >= H, bcum, g)
    eye = (lax.broadcasted_iota(jnp.int32, (L, L), 0) == lax.broadcasted_iota(jnp.int32, (L, L), 1)).astype(F32)
    c_all = lax.dot_general(eye, r_all, NT_DIMS, precision=HIGHEST, preferred_element_type=F32)

    ti = lax.broadcasted_iota(jnp.int32, (L, L), 0)
    si = lax.broadcasted_iota(jnp.int32, (L, L), 1)
    causal = si <= ti
    scale = dk ** -0.5

    c_old = [c_ref[h] for h in range(H)]
    n_old = [n_ref[h, 0:1, :] for h in range(H)]
    m_old = [m_ref[h, 0:1, 0:1] for h in range(H)]
    c_new, n_new, m_new_all = [], [], []

    heads = range(H)
    cols = [slice(h * dv, (h + 1) * dv) for h in heads]
    ig_s = [r_all[h:h + 1, :] for h in heads]
    b_s = [r_all[H + h:H + h + 1, :] for h in heads]
    ig_t = [c_all[:, h:h + 1] for h in heads]
    b_t = [c_all[:, H + h:H + h + 1] for h in heads]
    b_last = [b_s[h][:, L - 1:L] for h in heads]

    q, k, qb, kb, v = [], [], [], [], []
    for h in heads:
        acc = bconv_ref[:, cols[h]]
        for j in range(MLSTM_CONV):
            off = SUBLANES - (MLSTM_CONV - 1) + j
            acc = acc + wconv_ref[j:j + 1, cols[h]] * xbuf[off:off + L, cols[h]]
        xc = (acc * _sigmoid(acc)).astype(BF16)
        q.append(jnp.dot(xc, wq_ref[h], preferred_element_type=F32) * scale)
        k.append(jnp.dot(xc, wk_ref[h], preferred_element_type=F32))
        qb.append(q[h].astype(BF16))
        kb.append(k[h].astype(BF16))
        v.append(xm_ref[:, cols[h]])

    m_t, sc_inter, s = [], [], []
    for h in heads:
        dmat = jnp.where(causal, b_t[h] - b_s[h] + ig_s[h], -jnp.inf)
        inter = b_t[h] + m_old[h]
        m_t.append(jnp.maximum(inter, jnp.max(dmat, axis=1, keepdims=True)))
        wmat = jnp.exp(dmat - m_t[h])
        sc_inter.append(jnp.exp(inter - m_t[h]))
        s.append(lax.dot_general(qb[h], kb[h], NT_DIMS, preferred_element_type=F32) * wmat)

    for h in heads:
        num = (jnp.dot(s[h].astype(BF16), v[h], preferred_element_type=F32)
               + sc_inter[h] * jnp.dot(qb[h], c_old[h].astype(BF16), preferred_element_type=F32))
        den = (jnp.sum(s[h], axis=1, keepdims=True)
               + sc_inter[h] * jnp.sum(q[h] * n_old[h], axis=1, keepdims=True))
        out_ref[:, cols[h]] = (num / jnp.maximum(jnp.abs(den), jnp.exp(-m_t[h]))).astype(BF16)

    for h in heads:
        g_s = b_last[h] - b_s[h] + ig_s[h]
        m_new = jnp.maximum(b_last[h] + m_old[h], jnp.max(g_s, axis=1, keepdims=True))
        wk_t = jnp.exp(b_last[h] - b_t[h] + ig_t[h] - m_new)
        decay = jnp.exp(b_last[h] + m_old[h] - m_new)
        kw = k[h] * wk_t
        c_new.append(decay * c_old[h]
                     + lax.dot_general(kw.astype(BF16), v[h], TN_DIMS, preferred_element_type=F32))
        n_new.append(decay * n_old[h] + jnp.sum(kw, axis=0, keepdims=True))
        m_new_all.append(m_new)

    for h in range(H):
        c_ref[h] = c_new[h]
        n_ref[h, 0:1, :] = n_new[h]
        m_ref[h] = jnp.broadcast_to(m_new_all[h], m_ref.shape[1:])


def _mlstm(xm, gt, wconv, bconv, wq, wk, *, chunk):
    bsz, s, d = xm.shape
    H = MLSTM_HEADS
    dv = d // H
    dk = wq.shape[-1]
    ng = gt.shape[1]
    tok = lambda b, j: (b, j, 0)
    const2 = lambda b, j: (0, 0)
    const3 = lambda b, j: (0, 0, 0)
    return pl.pallas_call(
        functools.partial(_mlstm_kernel, chunk=chunk, dv=dv, dk=dk),
        grid=(bsz, s // chunk),
        in_specs=[pl.BlockSpec((None, chunk, d), tok),
                  pl.BlockSpec((None, ng, chunk), lambda b, j: (b, 0, j)),
                  pl.BlockSpec((MLSTM_CONV, d), const2),
                  pl.BlockSpec((1, d), const2),
                  pl.BlockSpec((H, dv, dk), const3),
                  pl.BlockSpec((H, dv, dk), const3)],
        out_specs=pl.BlockSpec((None, chunk, d), tok),
        out_shape=jax.ShapeDtypeStruct((bsz, s, d), BF16),
        scratch_shapes=[pltpu.VMEM((chunk + SUBLANES, d), F32),
                        pltpu.VMEM((H, dk, dv), F32),
                        pltpu.VMEM((H, SUBLANES, dk), F32),
                        pltpu.VMEM((H, SUBLANES, LANES), F32)],
        compiler_params=_params("arbitrary", "arbitrary"),
        name="mlstm",
    )(xm, gt, wconv, bconv, wq, wk)


def _s5_kernel(us_ref, ldt_ref, lamr_ref, lami_ref, lamc_ref, bt_ref, cc_ref, ca_ref, cb_ref, d_ref,
               out_ref, toep_ref, wb_ref, wc_ref, *, nb, nchunk):
    P = S5_STATE
    NC = S5_GROUP
    LS = LANES
    M = nb * nchunk

    dt = jnp.exp(ldt_ref[...])
    lam_re = lamr_ref[...]
    lam_im = lami_ref[...]
    lane2 = lax.broadcasted_iota(jnp.int32, (1, 2 * P), 1)
    sgn = jnp.where(lane2 < P, -1.0, 1.0)

    mag = jnp.exp(lam_re * dt)
    ar = mag * jnp.cos(lam_im * dt)
    ai = mag * jnp.sin(lam_im * dt)
    den = lam_re * lam_re + lam_im * lam_im
    zr = ((ar - 1.0) * lam_re + ai * lam_im) / den
    zi = (ai * lam_re - (ar - 1.0) * lam_im) / den
    b1 = bt_ref[...]
    bb = zr * b1 + zi * sgn * pltpu.roll(b1, P, 1)
    bb_sw = pltpu.roll(bb, P, 1)

    cr2 = cc_ref[0]
    ci2 = cc_ref[1]
    m12 = (cr2[:, None, :] * (bb * (-sgn))[None, :, :]
           - ci2[:, None, :] * bb_sw[None, :, :]).reshape(NC * NC, 2 * P)
    lam_re_c = lamc_ref[0]
    lam_im_c = lamc_ref[1]
    tau = lax.broadcasted_iota(jnp.int32, (2 * P, LS), 1).astype(F32)
    prow = lax.broadcasted_iota(jnp.int32, (2 * P, LS), 0)
    e0 = jnp.exp(lam_re_c * dt * tau)
    ang0 = lam_im_c * dt * tau
    pr0 = e0 * jnp.cos(ang0)
    pi0 = e0 * jnp.sin(ang0)
    pstack = jnp.where(prow < P, pr0, pi0)
    kpairs = jnp.dot(m12, pstack, precision=HIGHEST, preferred_element_type=F32)

    srow = lax.broadcasted_iota(jnp.int32, (LS, LS), 0)
    tcol = lax.broadcasted_iota(jnp.int32, (LS, LS), 1)
    lower = tcol >= srow
    srev = (LS - 1) - lax.broadcasted_iota(jnp.int32, (LS, 2 * P), 0).astype(F32)
    e1 = jnp.exp(lam_re * dt * srev)
    ang1 = lam_im * dt * srev
    r2 = e1 * jnp.cos(ang1)
    i2 = e1 * jnp.sin(ang1)
    bb_rot = bb_sw * sgn

    y = None
    x_end = None
    for cp0 in range(0, NC, S5_CBLOCK):
        for cp in range(cp0, cp0 + S5_CBLOCK):
            for c in range(NC):
                r = c * NC + cp
                kv = jnp.broadcast_to(kpairs[r:r + 1, :], (LS, LS))
                tz = pltpu.roll(kv, 0, 1, stride=1, stride_axis=0)
                toep_ref[cp * LS:(cp + 1) * LS, c * LS:(c + 1) * LS] = jnp.where(lower, tz, 0.0).astype(BF16)
            wb_ref[cp * LS:(cp + 1) * LS, :] = (r2 * bb[cp:cp + 1, :] + i2 * bb_rot[cp:cp + 1, :]).astype(BF16)
        rows = slice(cp0 * LS, (cp0 + S5_CBLOCK) * LS)
        ub = jnp.concatenate(
            [jnp.concatenate([us_ref[b, cp] for b in range(nb)], axis=0) for cp in range(cp0, cp0 + S5_CBLOCK)],
            axis=1)
        yp = jnp.dot(ub, toep_ref[rows, :], preferred_element_type=F32)
        xp = jnp.dot(ub, wb_ref[rows, :], preferred_element_type=F32)
        y = yp if y is None else y + yp
        x_end = xp if x_end is None else x_end + xp

    ar_c = jnp.exp(lam_re_c * dt) * jnp.cos(lam_im_c * dt)
    ai_c = jnp.exp(lam_re_c * dt) * jnp.sin(lam_im_c * dt)
    pr1 = pr0 * ar_c - pi0 * ai_c
    pi1 = pr0 * ai_c + pi0 * ar_c
    ca = ca_ref[...]
    cb = cb_ref[...]
    for c in range(NC):
        wc_ref[:, c * LS:(c + 1) * LS] = (ca[:, c:c + 1] * pr1 + cb[:, c:c + 1] * pi1).astype(BF16)

    jrow = lax.broadcasted_iota(jnp.int32, (M, 2 * P), 0) % nchunk
    lvl = lax.broadcasted_iota(jnp.int32, (SUBLANES, 2 * P), 0)
    nstep = (LS * jnp.left_shift(1, lvl)).astype(F32)
    el = jnp.exp(lam_re * dt * nstep)
    angl = lam_im * dt * nstep
    pl_all = el * jnp.cos(angl)
    ql_all = el * jnp.sin(angl) * sgn
    xs = x_end
    dstep = 1
    level = 0
    while dstep < nchunk:
        sh = jnp.where(jrow >= dstep, pltpu.roll(xs, dstep, 0), 0.0)
        xs = xs + pl_all[level:level + 1, :] * sh + ql_all[level:level + 1, :] * pltpu.roll(sh, P, 1)
        dstep *= 2
        level += 1
    x_prev = jnp.where(jrow >= 1, pltpu.roll(xs, 1, 0), 0.0)
    y = y + jnp.dot(x_prev.astype(BF16), wc_ref[...], preferred_element_type=F32)

    for c in range(NC):
        yc = y[:, c * LS:(c + 1) * LS]
        for b in range(nb):
            rows = slice(b * nchunk, (b + 1) * nchunk)
            yy = yc[rows, :] + d_ref[c:c + 1, 0:1] * us_ref[b, c].astype(F32)
            out_ref[b, c] = _gelu_tanh(yy).astype(BF16)


def _s5(ust4, ldt, lamr2, lami2, lamc, bt2, cc2, ca, cb, dcol):
    nb, w, nchunk, ls = ust4.shape
    G = w // S5_GROUP
    P2 = 2 * S5_STATE
    NC = S5_GROUP
    M = nb * nchunk
    g3 = lambda g: (g, 0, 0)
    g4 = lambda g: (g, 0, 0, 0)
    return pl.pallas_call(
        functools.partial(_s5_kernel, nb=nb, nchunk=nchunk),
        grid=(G,),
        in_specs=[pl.BlockSpec((nb, NC, nchunk, ls), lambda g: (0, g, 0, 0)),
                  pl.BlockSpec((None, 1, 1), g3),
                  pl.BlockSpec((None, 1, P2), g3),
                  pl.BlockSpec((None, 1, P2), g3),
                  pl.BlockSpec((None, 2, P2, 1), g4),
                  pl.BlockSpec((None, NC, P2), g3),
                  pl.BlockSpec((None, 2, NC, P2), g4),
                  pl.BlockSpec((None, P2, NC), g3),
                  pl.BlockSpec((None, P2, NC), g3),
                  pl.BlockSpec((None, NC, 1), g3)],
        out_specs=pl.BlockSpec((nb, NC, nchunk, ls), lambda g: (0, g, 0, 0)),
        out_shape=jax.ShapeDtypeStruct(ust4.shape, BF16),
        scratch_shapes=[pltpu.VMEM((NC * ls, NC * ls), BF16),
                        pltpu.VMEM((NC * ls, P2), BF16),
                        pltpu.VMEM((P2, NC * ls), BF16)],
        compiler_params=_params("arbitrary"),
        name="s5",
    )(ust4, ldt, lamr2, lami2, lamc, bt2, cc2, ca, cb, dcol)


def _tail_kernel(x_ref, mod_ref, hh_ref, som_ref, gain_ref, sga_ref, yst_ref, sgbt_ref, wdown_ref, wglut_ref,
                 wout_ref, g1_ref, b1_ref, x1_ref, h2_ref, *, d, alpha):
    dv = d // MLSTM_HEADS
    ya = jnp.concatenate(
        [(_standardize(hh_ref[:, h * dv:(h + 1) * dv].astype(F32)) * gain_ref[:, h * dv:(h + 1) * dv]
          * som_ref[:, h * dv:(h + 1) * dv].astype(F32)).astype(BF16) for h in range(MLSTM_HEADS)], axis=1)
    y_a = jnp.dot(ya, wdown_ref[...], preferred_element_type=F32)
    vgt = jnp.dot(wglut_ref[...], yst_ref[...], preferred_element_type=F32)
    zbt = vgt[0:d, :] * _sigmoid(vgt[d:2 * d, :]) * sgbt_ref[...].astype(F32)
    z = sga_ref[...].astype(F32) * y_a + zbt.T
    mix = jnp.dot(z.astype(BF16), wout_ref[...], preferred_element_type=F32)
    r = alpha * x_ref[...] + (1.0 + mod_ref[2:3, :]) * mix
    x1 = _standardize(r) * g1_ref[...] + b1_ref[...]
    x1_ref[...] = x1
    h2_ref[...] = (_standardize(x1) * (1.0 + mod_ref[4:5, :]) + mod_ref[3:4, :]).astype(BF16)


def _tail(x, mod, hh, som, gain, sga, yst, sgbt, wdown, wglut, wout, g1, b1, *, tm, alpha):
    bsz, s, d = x.shape
    s5w = yst.shape[1]
    tok = lambda b, i: (b, i, 0)
    chan = lambda b, i: (b, 0, i)
    const = lambda b, i: (0, 0)
    return pl.pallas_call(
        functools.partial(_tail_kernel, d=d, alpha=alpha),
        grid=(bsz, s // tm),
        in_specs=[pl.BlockSpec((None, tm, d), tok),
                  pl.BlockSpec((None, 6, d), lambda b, i: (b, 0, 0)),
                  pl.BlockSpec((None, tm, d), tok),
                  pl.BlockSpec((None, tm, d), tok),
                  pl.BlockSpec((1, d), const),
                  pl.BlockSpec((None, tm, d), tok),
                  pl.BlockSpec((None, s5w, tm), chan),
                  pl.BlockSpec((None, d, tm), chan),
                  pl.BlockSpec((d, d), const),
                  pl.BlockSpec((2 * d, s5w), const),
                  pl.BlockSpec((d, d), const),
                  pl.BlockSpec((1, d), const),
                  pl.BlockSpec((1, d), const)],
        out_specs=[pl.BlockSpec((None, tm, d), tok),
                   pl.BlockSpec((None, tm, d), tok)],
        out_shape=[jax.ShapeDtypeStruct((bsz, s, d), F32),
                   jax.ShapeDtypeStruct((bsz, s, d), BF16)],
        compiler_params=_params("arbitrary", "arbitrary"),
        name="tail",
    )(x, mod, hh, som, gain, sga, yst, sgbt, wdown, wglut, wout, g1, b1)


def _ffn_kernel(x1_ref, h2_ref, mod_ref, wup_ref, wconv_ref, bconv_ref, wdown_ref, g2_ref, b2_ref,
                out_ref, gbuf, *, hidden, hchunk, alpha):
    tm = h2_ref.shape[0]

    @pl.when(pl.program_id(1) == 0)
    def _():
        gbuf[tm:tm + SUBLANES, :] = jnp.zeros((SUBLANES, hidden), F32)

    h2 = h2_ref[...]
    acc = jnp.zeros((tm, out_ref.shape[1]), F32)
    for c0 in range(0, hidden, hchunk):
        cols = slice(c0, c0 + hchunk)
        val = jnp.dot(h2, wup_ref[:, cols], preferred_element_type=F32)
        gate = jnp.dot(h2, wup_ref[:, hidden + c0:hidden + c0 + hchunk], preferred_element_type=F32)
        gbuf[0:SUBLANES, cols] = gbuf[tm:tm + SUBLANES, cols]
        gbuf[SUBLANES:SUBLANES + tm, cols] = gate
        conv = bconv_ref[:, cols] + wconv_ref[FFN_CONV - 1:FFN_CONV, cols] * gate
        for k in range(1, FFN_CONV):
            conv = conv + wconv_ref[FFN_CONV - 1 - k:FFN_CONV - k, cols] * gbuf[SUBLANES - k:SUBLANES - k + tm, cols]
        act = (_gelu_tanh(conv) * val).astype(BF16)
        acc = acc + jnp.dot(act, wdown_ref[cols, :], preferred_element_type=F32)
    r = alpha * x1_ref[...] + (1.0 + mod_ref[5:6, :]) * acc
    out_ref[...] = _standardize(r) * g2_ref[...] + b2_ref[...]


def _ffn(x1, h2, mod, wup, wconv, bconv, wdown, g2, b2, *, tm, hchunk, alpha):
    bsz, s, d = x1.shape
    hidden = wdown.shape[0]
    tok = lambda b, i: (b, i, 0)
    const = lambda b, i: (0, 0)
    return pl.pallas_call(
        functools.partial(_ffn_kernel, hidden=hidden, hchunk=hchunk, alpha=alpha),
        grid=(bsz, s // tm),
        in_specs=[pl.BlockSpec((None, tm, d), tok),
                  pl.BlockSpec((None, tm, d), tok),
                  pl.BlockSpec((None, 6, d), lambda b, i: (b, 0, 0)),
                  pl.BlockSpec((d, 2 * hidden), const),
                  pl.BlockSpec((FFN_CONV, hidden), const),
                  pl.BlockSpec((1, hidden), const),
                  pl.BlockSpec((hidden, d), const),
                  pl.BlockSpec((1, d), const),
                  pl.BlockSpec((1, d), const)],
        out_specs=pl.BlockSpec((None, tm, d), tok),
        out_shape=jax.ShapeDtypeStruct((bsz, s, d), F32),
        scratch_shapes=[pltpu.VMEM((tm + SUBLANES, hidden), F32)],
        compiler_params=_params("arbitrary", "arbitrary"),
        name="ffn",
    )(x1, h2, mod, wup, wconv, bconv, wdown, g2, b2)


def _layer(x, mod, w_in, b_in, w_mlstm_conv, b_mlstm_conv, w_mlstm_q, w_mlstm_k, mlstm_norm_gain,
           w_mlstm_down, s5_lam_re, s5_lam_im, s5_log_dt, s5_b_re, s5_b_im, s5_c_re, s5_c_im, s5_d,
           w_s5_glu, w_mix_out, ln1_gain, ln1_bias, w_ffn_up, w_ffn_conv, b_ffn_conv, w_ffn_down,
           ln2_gain, ln2_bias, *, alpha):
    bsz, s, d = x.shape
    H = MLSTM_HEADS
    s5w = s5_d.shape[0]
    G = s5w // S5_GROUP
    tm = min(512, s)
    chunk = min(256, s)

    o_om, o_ip, o_fp, o_us = d, 2 * d, 2 * d + H, 2 * d + 2 * H
    o_ga, o_gb = o_us + s5w, o_us + s5w + d
    tok_cols = jnp.concatenate([jnp.arange(0, 2 * d), jnp.arange(o_ga, o_ga + d)])
    chan_cols = jnp.concatenate([jnp.arange(o_us, o_us + s5w), jnp.arange(o_gb, o_gb + d),
                                 jnp.arange(o_ip, o_ip + 2 * H)])
    wtok = w_in[:, tok_cols].astype(BF16)
    btok = b_in[tok_cols].reshape(1, -1)
    wt = w_in[:, chan_cols].T.astype(BF16)
    bt = b_in[chan_cols].reshape(-1, 1)

    xm, som, sga, ust, sgbt, gt = _inproj(x, mod, wtok, btok, wt, bt, tm=tm, s5w=s5w)

    hh = _mlstm(xm, gt, w_mlstm_conv, b_mlstm_conv.reshape(1, d), w_mlstm_q.astype(BF16),
                w_mlstm_k.astype(BF16), chunk=chunk)

    dup = lambda a: jnp.concatenate([a, a], axis=-1)
    lamr2 = dup(s5_lam_re)[:, None, :]
    lami2 = dup(s5_lam_im)[:, None, :]
    lamc = jnp.stack([dup(s5_lam_re), dup(s5_lam_im)], axis=1)[..., None]
    bt2 = jnp.concatenate([jnp.swapaxes(s5_b_re, 1, 2), jnp.swapaxes(s5_b_im, 1, 2)], axis=-1)
    cc2 = jnp.stack([dup(s5_c_re), dup(s5_c_im)], axis=1)
    crt = jnp.swapaxes(s5_c_re, 1, 2)
    cit = jnp.swapaxes(s5_c_im, 1, 2)
    ca = jnp.concatenate([crt, -cit], axis=1)
    cb = jnp.concatenate([-cit, -crt], axis=1)
    yst = _s5(ust.reshape(bsz, s5w, s // LANES, LANES), s5_log_dt.reshape(G, 1, 1), lamr2, lami2, lamc,
              bt2, cc2, ca, cb, s5_d.reshape(G, S5_GROUP, 1)).reshape(bsz, s5w, s)

    x1, h2 = _tail(x, mod, hh, som, mlstm_norm_gain.reshape(1, d), sga, yst, sgbt,
                   w_mlstm_down.astype(BF16), w_s5_glu.T.astype(BF16),
                   w_mix_out.astype(BF16), ln1_gain.reshape(1, d), ln1_bias.reshape(1, d), tm=tm, alpha=alpha)

    hidden = w_ffn_down.shape[0]
    return _ffn(x1, h2, mod, w_ffn_up.astype(BF16), w_ffn_conv, b_ffn_conv.reshape(1, hidden),
                w_ffn_down.astype(BF16), ln2_gain.reshape(1, d), ln2_bias.reshape(1, d),
                tm=tm, hchunk=hidden, alpha=alpha)


def kernel(x, c, w_ada, b_ada, w_in, b_in, w_mlstm_conv, b_mlstm_conv, w_mlstm_q, w_mlstm_k, mlstm_norm_gain, w_mlstm_down, s5_lam_re, s5_lam_im, s5_log_dt, s5_b_re, s5_b_im, s5_c_re, s5_c_im, s5_d, w_s5_glu, w_mix_out, ln1_gain, ln1_bias, w_ffn_up, w_ffn_conv, b_ffn_conv, w_ffn_down, ln2_gain, ln2_bias):
    depth = w_ada.shape[0]
    alpha = (2.0 * depth) ** 0.25
    bsz, d = c.shape
    for l in range(depth):
        mod = _adaln(c, w_ada[l], b_ada[l]).reshape(bsz, 6, d)
        x = _layer(x, mod, w_in[l], b_in[l], w_mlstm_conv[l], b_mlstm_conv[l], w_mlstm_q[l], w_mlstm_k[l],
                   mlstm_norm_gain[l], w_mlstm_down[l], s5_lam_re[l], s5_lam_im[l], s5_log_dt[l], s5_b_re[l],
                   s5_b_im[l], s5_c_re[l], s5_c_im[l], s5_d[l], w_s5_glu[l], w_mix_out[l], ln1_gain[l],
                   ln1_bias[l], w_ffn_up[l], w_ffn_conv[l], b_ffn_conv[l], w_ffn_down[l], ln2_gain[l],
                   ln2_bias[l], alpha=alpha)
    return x
```

```python
import functools
import math

import jax
import jax.numpy as jnp
from jax import lax
from jax.experimental import pallas as pl
from jax.experimental.pallas import tpu as pltpu

F32 = jnp.float32
BF16 = jnp.bfloat16
HIGHEST = lax.Precision.HIGHEST

LN_EPS = 1e-5
MLSTM_HEADS = 4
MLSTM_CONV = 4
FFN_CONV = 3
S5_GROUP = 16
S5_STATE = 64
S5_CBLOCK = 4
LANES = 128
SUBLANES = 8
VMEM_LIMIT_BYTES = 56 * 1024 * 1024

NT_DIMS = (((1,), (1,)), ((), ()))
TN_DIMS = (((0,), (0,)), ((), ()))


def _standardize(x):
    mu = jnp.mean(x, axis=-1, keepdims=True)
    xc = x - mu
    var = jnp.mean(xc * xc, axis=-1, keepdims=True)
    return xc * lax.rsqrt(var + LN_EPS)


def _sigmoid(x):
    return 1.0 / (1.0 + jnp.exp(-x))


def _gelu_tanh(x):
    return 0.5 * x * (1.0 + jnp.tanh(math.sqrt(2.0 / math.pi) * (x + 0.044715 * (x * x * x))))


def _params(*semantics):
    return pltpu.CompilerParams(dimension_semantics=semantics, vmem_limit_bytes=VMEM_LIMIT_BYTES)


def _adaln_kernel(c_ref, w_ref, b_ref, o_ref):
    c = c_ref[...]
    ca = c * _sigmoid(c)
    o_ref[...] = jnp.dot(ca, w_ref[...], precision=HIGHEST, preferred_element_type=F32) + b_ref[...]


def _adaln(c, w, b):
    bsz, d = c.shape
    n = w.shape[1]
    return pl.pallas_call(
        _adaln_kernel,
        grid=(n // d,),
        in_specs=[pl.BlockSpec((bsz, d), lambda j: (0, 0)),
                  pl.BlockSpec((d, d), lambda j: (0, j)),
                  pl.BlockSpec((1, d), lambda j: (0, j))],
        out_specs=pl.BlockSpec((bsz, d), lambda j: (0, j)),
        out_shape=jax.ShapeDtypeStruct((bsz, n), F32),
        compiler_params=_params("arbitrary"),
        name="adaln",
    )(c, w, b.reshape(1, n))


def _log_sigmoid(g):
    return jnp.minimum(g, 0.0) - jnp.log(1.0 + jnp.exp(-jnp.abs(g)))


def _inproj_kernel(x_ref, mod_ref, wtok_ref, btok_ref, wt_ref, bt_ref,
                   xm_ref, som_ref, sga_ref, ust_ref, sgbt_ref, gt_ref, *, d, s5w):
    x = x_ref[...]
    h = (_standardize(x) * (1.0 + mod_ref[1:2, :]) + mod_ref[0:1, :]).astype(BF16)
    p = jnp.dot(h, wtok_ref[:, 0:d], preferred_element_type=F32) + btok_ref[:, 0:d]
    xm_ref[...] = p.astype(BF16)
    p = jnp.dot(h, wtok_ref[:, d:2 * d], preferred_element_type=F32) + btok_ref[:, d:2 * d]
    som_ref[...] = _sigmoid(p).astype(BF16)
    p = jnp.dot(h, wtok_ref[:, 2 * d:3 * d], preferred_element_type=F32) + btok_ref[:, 2 * d:3 * d]
    sga_ref[...] = _sigmoid(p).astype(BF16)
    pt = lax.dot_general(wt_ref[0:s5w, :], h, NT_DIMS, preferred_element_type=F32) + bt_ref[0:s5w, :]
    ust_ref[...] = pt.astype(BF16)
    pt = lax.dot_general(wt_ref[s5w:s5w + d, :], h, NT_DIMS, preferred_element_type=F32) + bt_ref[s5w:s5w + d, :]
    sgbt_ref[...] = _sigmoid(pt).astype(BF16)
    ng = 2 * MLSTM_HEADS
    pt = lax.dot_general(wt_ref[s5w + d:s5w + d + ng, :], h, NT_DIMS, preferred_element_type=F32)
    gt_ref[...] = pt + bt_ref[s5w + d:s5w + d + ng, :]


def _inproj(x, mod, wtok, btok, wt, bt, *, tm, s5w):
    bsz, s, d = x.shape
    nt = wt.shape[0]
    ng = 2 * MLSTM_HEADS
    tok = lambda b, i: (b, i, 0)
    chan = lambda b, i: (b, 0, i)
    const = lambda b, i: (0, 0)
    return pl.pallas_call(
        functools.partial(_inproj_kernel, d=d, s5w=s5w),
        grid=(bsz, s // tm),
        in_specs=[pl.BlockSpec((None, tm, d), tok),
                  pl.BlockSpec((None, 6, d), lambda b, i: (b, 0, 0)),
                  pl.BlockSpec((d, 3 * d), const),
                  pl.BlockSpec((1, 3 * d), const),
                  pl.BlockSpec((nt, d), const),
                  pl.BlockSpec((nt, 1), const)],
        out_specs=[pl.BlockSpec((None, tm, d), tok),
                   pl.BlockSpec((None, tm, d), tok),
                   pl.BlockSpec((None, tm, d), tok),
                   pl.BlockSpec((None, s5w, tm), chan),
                   pl.BlockSpec((None, d, tm), chan),
                   pl.BlockSpec((None, ng, tm), chan)],
        out_shape=[jax.ShapeDtypeStruct((bsz, s, d), BF16),
                   jax.ShapeDtypeStruct((bsz, s, d), BF16),
                   jax.ShapeDtypeStruct((bsz, s, d), BF16),
                   jax.ShapeDtypeStruct((bsz, s5w, s), BF16),
                   jax.ShapeDtypeStruct((bsz, d, s), BF16),
                   jax.ShapeDtypeStruct((bsz, ng, s), F32)],
        compiler_params=_params("arbitrary", "arbitrary"),
        name="inproj",
    )(x, mod, wtok, btok, wt, bt)


def _lane_cumsum(x):
    n = x.shape[-1]
    lane = lax.broadcasted_iota(jnp.int32, x.shape, x.ndim - 1)
    sh = 1
    while sh < n:
        x = x + jnp.where(lane >= sh, pltpu.roll(x, sh, x.ndim - 1), 0.0)
        sh *= 2
    return x


def _mlstm_kernel(xm_ref, gt_ref, wconv_ref, bconv_ref, wq_ref, wk_ref,
                  out_ref, xbuf, c_ref, n_ref, m_ref, *, chunk, dv, dk):
    L = chunk
    H = MLSTM_HEADS

    @pl.when(pl.program_id(1) == 0)
    def _():
        xbuf[L:L + SUBLANES, :] = jnp.zeros((SUBLANES, xbuf.shape[1]), F32)
        c_ref[...] = jnp.zeros(c_ref.shape, F32)
        n_ref[...] = jnp.zeros(n_ref.shape, F32)
        m_ref[...] = jnp.zeros(m_ref.shape, F32)

    xbuf[0:SUBLANES, :] = xbuf[L:L + SUBLANES, :]
    xbuf[SUBLANES:SUBLANES + L, :] = xm_ref[...].astype(F32)

    ti = lax.broadcasted_iota(jnp.int32, (L, L), 0)
    si = lax.broadcasted_iota(jnp.int32, (L, L), 1)
    causal = si <= ti
    scale = dk ** -0.5

    c_old = [c_ref[h] for h in range(H)]
    n_old = [n_ref[h, 0:1, :] for h in range(H)]
    m_old = [m_ref[h, 0:1, 0:1] for h in range(H)]
    c_new, n_new, m_new_all = [], [], []

    heads = range(H)
    cols = [slice(h * dv, (h + 1) * dv) for h in heads]

    g = gt_ref[...]
    row = lax.broadcasted_iota(jnp.int32, g.shape, 0)
    bcum = _lane_cumsum(jnp.where(row >= H, _log_sigmoid(g), 0.0))
    r_all = jnp.where(row >= H, bcum, g)
    eye = (ti == si).astype(F32)
    c_all = lax.dot_general(eye, r_all, NT_DIMS, precision=HIGHEST, preferred_element_type=F32)
    ig_s = [r_all[h:h + 1, :] for h in heads]
    b_s = [r_all[H + h:H + h + 1, :] for h in heads]
    ig_t = [c_all[:, h:h + 1] for h in heads]
    b_t = [c_all[:, H + h:H + h + 1] for h in heads]
    b_last = [b_s[h][:, L - 1:L] for h in heads]

    xc, q, k, qb, kb, v = [], [], [], [], [], []
    for h in heads:
        acc = bconv_ref[:, cols[h]]
        for j in range(MLSTM_CONV):
            off = SUBLANES - (MLSTM_CONV - 1) + j
            acc = acc + wconv_ref[j:j + 1, cols[h]] * xbuf[off:off + L, cols[h]]
        xc.append((acc * _sigmoid(acc)).astype(BF16))
    for h in heads:
        q.append(jnp.dot(xc[h], wq_ref[h], preferred_element_type=F32) * scale)
        k.append(jnp.dot(xc[h], wk_ref[h], preferred_element_type=F32))
        qb.append(q[h].astype(BF16))
        kb.append(k[h].astype(BF16))
        v.append(xm_ref[:, cols[h]])

    m_t, sc_inter, wmat, s = [], [], [], []
    for h in heads:
        dmat = jnp.where(causal, b_t[h] - b_s[h] + ig_s[h], -jnp.inf)
        inter = b_t[h] + m_old[h]
        m_t.append(jnp.maximum(inter, jnp.max(dmat, axis=1, keepdims=True)))
        wmat.append(jnp.exp(dmat - m_t[h]))
        sc_inter.append(jnp.exp(inter - m_t[h]))
    for h in heads:
        s.append(lax.dot_general(qb[h], kb[h], NT_DIMS, preferred_element_type=F32) * wmat[h])

    for h in heads:
        g_s = b_last[h] - b_s[h] + ig_s[h]
        m_new = jnp.maximum(b_last[h] + m_old[h], jnp.max(g_s, axis=1, keepdims=True))
        wk_t = jnp.exp(b_last[h] - b_t[h] + ig_t[h] - m_new)
        decay = jnp.exp(b_last[h] + m_old[h] - m_new)
        kw = k[h] * wk_t
        c_new.append(decay * c_old[h]
                     + lax.dot_general(kw.astype(BF16), v[h], TN_DIMS, preferred_element_type=F32))
        n_new.append(decay * n_old[h] + jnp.sum(kw, axis=0, keepdims=True))
        m_new_all.append(m_new)

    for h in heads:
        num = (jnp.dot(s[h].astype(BF16), v[h], preferred_element_type=F32)
               + sc_inter[h] * jnp.dot(qb[h], c_old[h].astype(BF16), preferred_element_type=F32))
        den = (jnp.sum(s[h], axis=1, keepdims=True)
               + sc_inter[h] * jnp.sum(q[h] * n_old[h], axis=1, keepdims=True))
        out_ref[:, cols[h]] = (num / jnp.maximum(jnp.abs(den), jnp.exp(-m_t[h]))).astype(BF16)

    for h in range(H):
        c_ref[h] = c_new[h]
        n_ref[h, 0:1, :] = n_new[h]
        m_ref[h] = jnp.broadcast_to(m_new_all[h], m_ref.shape[1:])


def _mlstm(xm, gt, wconv, bconv, wq, wk, *, chunk):
    bsz, s, d = xm.shape
    H = MLSTM_HEADS
    dv = d // H
    dk = wq.shape[-1]
    ng = gt.shape[1]
    tok = lambda b, j: (b, j, 0)
    const2 = lambda b, j: (0, 0)
    const3 = lambda b, j: (0, 0, 0)
    return pl.pallas_call(
        functools.partial(_mlstm_kernel, chunk=chunk, dv=dv, dk=dk),
        grid=(bsz, s // chunk),
        in_specs=[pl.BlockSpec((None, chunk, d), tok),
                  pl.BlockSpec((None, ng, chunk), lambda b, j: (b, 0, j)),
                  pl.BlockSpec((MLSTM_CONV, d), const2),
                  pl.BlockSpec((1, d), const2),
                  pl.BlockSpec((H, dv, dk), const3),
                  pl.BlockSpec((H, dv, dk), const3)],
        out_specs=pl.BlockSpec((None, chunk, d), tok),
        out_shape=jax.ShapeDtypeStruct((bsz, s, d), BF16),
        scratch_shapes=[pltpu.VMEM((chunk + SUBLANES, d), F32),
                        pltpu.VMEM((H, dk, dv), F32),
                        pltpu.VMEM((H, SUBLANES, dk), F32),
                        pltpu.VMEM((H, SUBLANES, LANES), F32)],
        compiler_params=_params("arbitrary", "arbitrary"),
        name="mlstm",
    )(xm, gt, wconv, bconv, wq, wk)


def _s5_kernel(us_ref, ldt_ref, lamr_ref, lami_ref, lamc_ref, bt_ref, cc_ref, ca_ref, cb_ref, d_ref,
               out_ref, toep_ref, wb_ref, wc_ref, *, nb, nchunk):
    P = S5_STATE
    NC = S5_GROUP
    LS = LANES
    M = nb * nchunk

    dt = jnp.exp(ldt_ref[...])
    lam_re = lamr_ref[...]
    lam_im = lami_ref[...]
    lane2 = lax.broadcasted_iota(jnp.int32, (1, 2 * P), 1)
    sgn = jnp.where(lane2 < P, -1.0, 1.0)

    mag = jnp.exp(lam_re * dt)
    ar = mag * jnp.cos(lam_im * dt)
    ai = mag * jnp.sin(lam_im * dt)
    den = lam_re * lam_re + lam_im * lam_im
    zr = ((ar - 1.0) * lam_re + ai * lam_im) / den
    zi = (ai * lam_re - (ar - 1.0) * lam_im) / den
    b1 = bt_ref[...]
    bb = zr * b1 + zi * sgn * pltpu.roll(b1, P, 1)
    bb_sw = pltpu.roll(bb, P, 1)

    cr2 = cc_ref[0]
    ci2 = cc_ref[1]
    m12 = (cr2[:, None, :] * (bb * (-sgn))[None, :, :]
           - ci2[:, None, :] * bb_sw[None, :, :]).reshape(NC * NC, 2 * P)
    lam_re_c = lamc_ref[0]
    lam_im_c = lamc_ref[1]
    tau = lax.broadcasted_iota(jnp.int32, (2 * P, LS), 1).astype(F32)
    prow = lax.broadcasted_iota(jnp.int32, (2 * P, LS), 0)
    e0 = jnp.exp(lam_re_c * dt * tau)
    ang0 = lam_im_c * dt * tau
    pr0 = e0 * jnp.cos(ang0)
    pi0 = e0 * jnp.sin(ang0)
    pstack = jnp.where(prow < P, pr0, pi0)
    kpairs = jnp.dot(m12, pstack, precision=HIGHEST, preferred_element_type=F32)

    srow = lax.broadcasted_iota(jnp.int32, (LS, LS), 0)
    tcol = lax.broadcasted_iota(jnp.int32, (LS, LS), 1)
    lower = tcol >= srow
    srev = (LS - 1) - lax.broadcasted_iota(jnp.int32, (LS, 2 * P), 0).astype(F32)
    e1 = jnp.exp(lam_re * dt * srev)
    ang1 = lam_im * dt * srev
    r2 = e1 * jnp.cos(ang1)
    i2 = e1 * jnp.sin(ang1)
    bb_rot = bb_sw * sgn

    y = None
    x_end = None
    for cp0 in range(0, NC, S5_CBLOCK):
        for cp in range(cp0, cp0 + S5_CBLOCK):
            for c in range(NC):
                r = c * NC + cp
                kv = jnp.broadcast_to(kpairs[r:r + 1, :], (LS, LS))
                tz = pltpu.roll(kv, 0, 1, stride=1, stride_axis=0)
                toep_ref[cp * LS:(cp + 1) * LS, c * LS:(c + 1) * LS] = jnp.where(lower, tz, 0.0).astype(BF16)
            wb_ref[cp * LS:(cp + 1) * LS, :] = (r2 * bb[cp:cp + 1, :] + i2 * bb_rot[cp:cp + 1, :]).astype(BF16)
        rows = slice(cp0 * LS, (cp0 + S5_CBLOCK) * LS)
        ub = jnp.concatenate(
            [jnp.concatenate([us_ref[b, cp] for b in range(nb)], axis=0) for cp in range(cp0, cp0 + S5_CBLOCK)],
            axis=1)
        yp = jnp.dot(ub, toep_ref[rows, :], preferred_element_type=F32)
        xp = jnp.dot(ub, wb_ref[rows, :], preferred_element_type=F32)
        y = yp if y is None else y + yp
        x_end = xp if x_end is None else x_end + xp

    ar_c = jnp.exp(lam_re_c * dt) * jnp.cos(lam_im_c * dt)
    ai_c = jnp.exp(lam_re_c * dt) * jnp.sin(lam_im_c * dt)
    pr1 = pr0 * ar_c - pi0 * ai_c
    pi1 = pr0 * ai_c + pi0 * ar_c
    ca = ca_ref[...]
    cb = cb_ref[...]
    for c in range(NC):
        wc_ref[:, c * LS:(c + 1) * LS] = (ca[:, c:c + 1] * pr1 + cb[:, c:c + 1] * pi1).astype(BF16)

    jrow = lax.broadcasted_iota(jnp.int32, (M, 2 * P), 0) % nchunk
    lvl = lax.broadcasted_iota(jnp.int32, (SUBLANES, 2 * P), 0)
    nstep = (LS * jnp.left_shift(1, lvl)).astype(F32)
    el = jnp.exp(lam_re * dt * nstep)
    angl = lam_im * dt * nstep
    pl_all = el * jnp.cos(angl)
    ql_all = el * jnp.sin(angl) * sgn
    xs = x_end
    dstep = 1
    level = 0
    while dstep < nchunk:
        sh = jnp.where(jrow >= dstep, pltpu.roll(xs, dstep, 0), 0.0)
        xs = xs + pl_all[level:level + 1, :] * sh + ql_all[level:level + 1, :] * pltpu.roll(sh, P, 1)
        dstep *= 2
        level += 1
    x_prev = jnp.where(jrow >= 1, pltpu.roll(xs, 1, 0), 0.0)
    y = y + jnp.dot(x_prev.astype(BF16), wc_ref[...], preferred_element_type=F32)

    for c in range(NC):
        yc = y[:, c * LS:(c + 1) * LS]
        for b in range(nb):
            rows = slice(b * nchunk, (b + 1) * nchunk)
            yy = yc[rows, :] + d_ref[c:c + 1, 0:1] * us_ref[b, c].astype(F32)
            out_ref[b, c] = _gelu_tanh(yy).astype(BF16)


def _s5(ust4, ldt, lamr2, lami2, lamc, bt2, cc2, ca, cb, dcol):
    nb, w, nchunk, ls = ust4.shape
    G = w // S5_GROUP
    P2 = 2 * S5_STATE
    NC = S5_GROUP
    M = nb * nchunk
    g3 = lambda g: (g, 0, 0)
    g4 = lambda g: (g, 0, 0, 0)
    return pl.pallas_call(
        functools.partial(_s5_kernel, nb=nb, nchunk=nchunk),
        grid=(G,),
        in_specs=[pl.BlockSpec((nb, NC, nchunk, ls), lambda g: (0, g, 0, 0)),
                  pl.BlockSpec((None, 1, 1), g3),
                  pl.BlockSpec((None, 1, P2), g3),
                  pl.BlockSpec((None, 1, P2), g3),
                  pl.BlockSpec((None, 2, P2, 1), g4),
                  pl.BlockSpec((None, NC, P2), g3),
                  pl.BlockSpec((None, 2, NC, P2), g4),
                  pl.BlockSpec((None, P2, NC), g3),
                  pl.BlockSpec((None, P2, NC), g3),
                  pl.BlockSpec((None, NC, 1), g3)],
        out_specs=pl.BlockSpec((nb, NC, nchunk, ls), lambda g: (0, g, 0, 0)),
        out_shape=jax.ShapeDtypeStruct(ust4.shape, BF16),
        scratch_shapes=[pltpu.VMEM((NC * ls, NC * ls), BF16),
                        pltpu.VMEM((NC * ls, P2), BF16),
                        pltpu.VMEM((P2, NC * ls), BF16)],
        compiler_params=_params("arbitrary"),
        name="s5",
    )(ust4, ldt, lamr2, lami2, lamc, bt2, cc2, ca, cb, dcol)


def _tail_kernel(x_ref, mod_ref, hh_ref, som_ref, gain_ref, sga_ref, yst_ref, sgbt_ref, wdown_ref, wglut_ref,
                 wout_ref, g1_ref, b1_ref, x1_ref, h2_ref, *, d, alpha):
    dv = d // MLSTM_HEADS
    vgt = jnp.dot(wglut_ref[...], yst_ref[...], preferred_element_type=F32)
    ya = jnp.concatenate(
        [(_standardize(hh_ref[:, h * dv:(h + 1) * dv].astype(F32)) * gain_ref[:, h * dv:(h + 1) * dv]
          * som_ref[:, h * dv:(h + 1) * dv].astype(F32)).astype(BF16) for h in range(MLSTM_HEADS)], axis=1)
    y_a = jnp.dot(ya, wdown_ref[...], preferred_element_type=F32)
    zbt = vgt[0:d, :] * _sigmoid(vgt[d:2 * d, :]) * sgbt_ref[...].astype(F32)
    z = sga_ref[...].astype(F32) * y_a + zbt.T
    mix = jnp.dot(z.astype(BF16), wout_ref[...], preferred_element_type=F32)
    r = alpha * x_ref[...] + (1.0 + mod_ref[2:3, :]) * mix
    x1 = _standardize(r) * g1_ref[...] + b1_ref[...]
    x1_ref[...] = x1
    h2_ref[...] = (_standardize(x1) * (1.0 + mod_ref[4:5, :]) + mod_ref[3:4, :]).astype(BF16)


def _tail(x, mod, hh, som, gain, sga, yst, sgbt, wdown, wglut, wout, g1, b1, *, tm, alpha):
    bsz, s, d = x.shape
    s5w = yst.shape[1]
    tok = lambda b, i: (b, i, 0)
    chan = lambda b, i: (b, 0, i)
    const = lambda b, i: (0, 0)
    return pl.pallas_call(
        functools.partial(_tail_kernel, d=d, alpha=alpha),
        grid=(bsz, s // tm),
        in_specs=[pl.BlockSpec((None, tm, d), tok),
                  pl.BlockSpec((None, 6, d), lambda b, i: (b, 0, 0)),
                  pl.BlockSpec((None, tm, d), tok),
                  pl.BlockSpec((None, tm, d), tok),
                  pl.BlockSpec((1, d), const),
                  pl.BlockSpec((None, tm, d), tok),
                  pl.BlockSpec((None, s5w, tm), chan),
                  pl.BlockSpec((None, d, tm), chan),
                  pl.BlockSpec((d, d), const),
                  pl.BlockSpec((2 * d, s5w), const),
                  pl.BlockSpec((d, d), const),
                  pl.BlockSpec((1, d), const),
                  pl.BlockSpec((1, d), const)],
        out_specs=[pl.BlockSpec((None, tm, d), tok),
                   pl.BlockSpec((None, tm, d), tok)],
        out_shape=[jax.ShapeDtypeStruct((bsz, s, d), F32),
                   jax.ShapeDtypeStruct((bsz, s, d), BF16)],
        compiler_params=_params("arbitrary", "arbitrary"),
        name="tail",
    )(x, mod, hh, som, gain, sga, yst, sgbt, wdown, wglut, wout, g1, b1)


def _ffn_kernel(x1_ref, h2_ref, mod_ref, wup_ref, wconv_ref, bconv_ref, wdown_ref, g2_ref, b2_ref,
                out_ref, gbuf, *, hidden, hchunk, alpha):
    tm = h2_ref.shape[0]

    @pl.when(pl.program_id(1) == 0)
    def _():
        gbuf[tm:tm + SUBLANES, :] = jnp.zeros((SUBLANES, hidden), F32)

    h2 = h2_ref[...]
    acc = jnp.zeros((tm, out_ref.shape[1]), F32)
    for c0 in range(0, hidden, hchunk):
        c1 = min(c0 + hchunk, hidden)
        cols = slice(c0, c1)
        gate = jnp.dot(h2, wup_ref[:, hidden + c0:hidden + c1], preferred_element_type=F32)
        gbuf[0:SUBLANES, cols] = gbuf[tm:tm + SUBLANES, cols]
        gbuf[SUBLANES:SUBLANES + tm, cols] = gate
        conv = bconv_ref[:, cols] + wconv_ref[FFN_CONV - 1:FFN_CONV, cols] * gate
        for k in range(1, FFN_CONV):
            conv = conv + wconv_ref[FFN_CONV - 1 - k:FFN_CONV - k, cols] * gbuf[SUBLANES - k:SUBLANES - k + tm, cols]
        gact = _gelu_tanh(conv)
        val = jnp.dot(h2, wup_ref[:, cols], preferred_element_type=F32)
        act = (gact * val).astype(BF16)
        acc = acc + jnp.dot(act, wdown_ref[cols, :], preferred_element_type=F32)
    r = alpha * x1_ref[...] + (1.0 + mod_ref[5:6, :]) * acc
    out_ref[...] = _standardize(r) * g2_ref[...] + b2_ref[...]


def _ffn(x1, h2, mod, wup, wconv, bconv, wdown, g2, b2, *, tm, hchunk, alpha):
    bsz, s, d = x1.shape
    hidden = wdown.shape[0]
    tok = lambda b, i: (b, i, 0)
    const = lambda b, i: (0, 0)
    return pl.pallas_call(
        functools.partial(_ffn_kernel, hidden=hidden, hchunk=hchunk, alpha=alpha),
        grid=(bsz, s // tm),
        in_specs=[pl.BlockSpec((None, tm, d), tok),
                  pl.BlockSpec((None, tm, d), tok),
                  pl.BlockSpec((None, 6, d), lambda b, i: (b, 0, 0)),
                  pl.BlockSpec((d, 2 * hidden), const),
                  pl.BlockSpec((FFN_CONV, hidden), const),
                  pl.BlockSpec((1, hidden), const),
                  pl.BlockSpec((hidden, d), const),
                  pl.BlockSpec((1, d), const),
                  pl.BlockSpec((1, d), const)],
        out_specs=pl.BlockSpec((None, tm, d), tok),
        out_shape=jax.ShapeDtypeStruct((bsz, s, d), F32),
        scratch_shapes=[pltpu.VMEM((tm + SUBLANES, hidden), F32)],
        compiler_params=_params("arbitrary", "arbitrary"),
        name="ffn",
    )(x1, h2, mod, wup, wconv, bconv, wdown, g2, b2)


def _layer(x, mod, w_in, b_in, w_mlstm_conv, b_mlstm_conv, w_mlstm_q, w_mlstm_k, mlstm_norm_gain,
           w_mlstm_down, s5_lam_re, s5_lam_im, s5_log_dt, s5_b_re, s5_b_im, s5_c_re, s5_c_im, s5_d,
           w_s5_glu, w_mix_out, ln1_gain, ln1_bias, w_ffn_up, w_ffn_conv, b_ffn_conv, w_ffn_down,
           ln2_gain, ln2_bias, *, alpha):
    bsz, s, d = x.shape
    H = MLSTM_HEADS
    s5w = s5_d.shape[0]
    G = s5w // S5_GROUP
    tm = min(512, s)
    chunk = min(256, s)

    o_om, o_ip, o_fp, o_us = d, 2 * d, 2 * d + H, 2 * d + 2 * H
    o_ga, o_gb = o_us + s5w, o_us + s5w + d
    tok_cols = jnp.concatenate([jnp.arange(0, 2 * d), jnp.arange(o_ga, o_ga + d)])
    chan_cols = jnp.concatenate([jnp.arange(o_us, o_us + s5w), jnp.arange(o_gb, o_gb + d),
                                 jnp.arange(o_ip, o_ip + 2 * H)])
    wtok = w_in[:, tok_cols].astype(BF16)
    btok = b_in[tok_cols].reshape(1, -1)
    wt = w_in[:, chan_cols].T.astype(BF16)
    bt = b_in[chan_cols].reshape(-1, 1)

    xm, som, sga, ust, sgbt, gt = _inproj(x, mod, wtok, btok, wt, bt, tm=tm, s5w=s5w)

    hh = _mlstm(xm, gt, w_mlstm_conv, b_mlstm_conv.reshape(1, d), w_mlstm_q.astype(BF16),
                w_mlstm_k.astype(BF16), chunk=chunk)

    dup = lambda a: jnp.concatenate([a, a], axis=-1)
    lamr2 = dup(s5_lam_re)[:, None, :]
    lami2 = dup(s5_lam_im)[:, None, :]
    lamc = jnp.stack([dup(s5_lam_re), dup(s5_lam_im)], axis=1)[..., None]
    bt2 = jnp.concatenate([jnp.swapaxes(s5_b_re, 1, 2), jnp.swapaxes(s5_b_im, 1, 2)], axis=-1)
    cc2 = jnp.stack([dup(s5_c_re), dup(s5_c_im)], axis=1)
    crt = jnp.swapaxes(s5_c_re, 1, 2)
    cit = jnp.swapaxes(s5_c_im, 1, 2)
    ca = jnp.concatenate([crt, -cit], axis=1)
    cb = jnp.concatenate([-cit, -crt], axis=1)
    yst = _s5(ust.reshape(bsz, s5w, s // LANES, LANES), s5_log_dt.reshape(G, 1, 1), lamr2, lami2, lamc,
              bt2, cc2, ca, cb, s5_d.reshape(G, S5_GROUP, 1)).reshape(bsz, s5w, s)

    x1, h2 = _tail(x, mod, hh, som, mlstm_norm_gain.reshape(1, d), sga, yst, sgbt,
                   w_mlstm_down.astype(BF16), w_s5_glu.T.astype(BF16),
                   w_mix_out.astype(BF16), ln1_gain.reshape(1, d), ln1_bias.reshape(1, d), tm=tm, alpha=alpha)

    hidden = w_ffn_down.shape[0]
    return _ffn(x1, h2, mod, w_ffn_up.astype(BF16), w_ffn_conv, b_ffn_conv.reshape(1, hidden),
                w_ffn_down.astype(BF16), ln2_gain.reshape(1, d), ln2_bias.reshape(1, d),
                tm=tm, hchunk=hidden, alpha=alpha)


def kernel(x, c, w_ada, b_ada, w_in, b_in, w_mlstm_conv, b_mlstm_conv, w_mlstm_q, w_mlstm_k, mlstm_norm_gain, w_mlstm_down, s5_lam_re, s5_lam_im, s5_log_dt, s5_b_re, s5_b_im, s5_c_re, s5_c_im, s5_d, w_s5_glu, w_mix_out, ln1_gain, ln1_bias, w_ffn_up, w_ffn_conv, b_ffn_conv, w_ffn_down, ln2_gain, ln2_bias):
    depth = w_ada.shape[0]
    alpha = (2.0 * depth) ** 0.25
    bsz, d = c.shape
    for l in range(depth):
        mod = _adaln(c, w_ada[l], b_ada[l]).reshape(bsz, 6, d)
        x = _layer(x, mod, w_in[l], b_in[l], w_mlstm_conv[l], b_mlstm_conv[l], w_mlstm_q[l], w_mlstm_k[l],
                   mlstm_norm_gain[l], w_mlstm_down[l], s5_lam_re[l], s5_lam_im[l], s5_log_dt[l], s5_b_re[l],
                   s5_b_im[l], s5_c_re[l], s5_c_im[l], s5_d[l], w_s5_glu[l], w_mix_out[l], ln1_gain[l],
                   ln1_bias[l], w_ffn_up[l], w_ffn_conv[l], b_ffn_conv[l], w_ffn_down[l], ln2_gain[l],
                   ln2_bias[l], alpha=alpha)
    return x
```

```python
import functools
import math

import jax
import jax.numpy as jnp
from jax import lax
from jax.experimental import pallas as pl
from jax.experimental.pallas import tpu as pltpu

F32 = jnp.float32
BF16 = jnp.bfloat16
HIGHEST = lax.Precision.HIGHEST

LN_EPS = 1e-5
MLSTM_HEADS = 4
MLSTM_CONV = 4
FFN_CONV = 3
S5_GROUP = 16
S5_STATE = 64
S5_CBLOCK = 2
TAIL_SUBTILES = 2
FFN_SUBTILES = 2
LANES = 128
SUBLANES = 8
VMEM_LIMIT_BYTES = 56 * 1024 * 1024

NT_DIMS = (((1,), (1,)), ((), ()))
TN_DIMS = (((0,), (0,)), ((), ()))


def _standardize(x):
    mu = jnp.mean(x, axis=-1, keepdims=True)
    xc = x - mu
    var = jnp.mean(xc * xc, axis=-1, keepdims=True)
    return xc * lax.rsqrt(var + LN_EPS)


def _sigmoid(x):
    return 1.0 / (1.0 + jnp.exp(-x))


def _gelu_tanh(x):
    return 0.5 * x * (1.0 + jnp.tanh(math.sqrt(2.0 / math.pi) * (x + 0.044715 * (x * x * x))))


def _params(*semantics):
    return pltpu.CompilerParams(dimension_semantics=semantics, vmem_limit_bytes=VMEM_LIMIT_BYTES)


def _adaln_kernel(c_ref, w_ref, b_ref, o_ref):
    c = c_ref[...]
    ca = c * _sigmoid(c)
    o_ref[...] = jnp.dot(ca, w_ref[...], precision=HIGHEST, preferred_element_type=F32) + b_ref[...]


def _adaln(c, w, b):
    bsz, d = c.shape
    n = w.shape[1]
    return pl.pallas_call(
        _adaln_kernel,
        grid=(n // d,),
        in_specs=[pl.BlockSpec((bsz, d), lambda j: (0, 0)),
                  pl.BlockSpec((d, d), lambda j: (0, j)),
                  pl.BlockSpec((1, d), lambda j: (0, j))],
        out_specs=pl.BlockSpec((bsz, d), lambda j: (0, j)),
        out_shape=jax.ShapeDtypeStruct((bsz, n), F32),
        compiler_params=_params("arbitrary"),
        name="adaln",
    )(c, w, b.reshape(1, n))


def _log_sigmoid(g):
    return jnp.minimum(g, 0.0) - jnp.log(1.0 + jnp.exp(-jnp.abs(g)))


def _inproj_kernel(x_ref, mod_ref, wtok_ref, btok_ref, wt_ref, bt_ref,
                   xm_ref, som_ref, sga_ref, ust_ref, sgbt_ref, gt_ref, *, d, s5w):
    x = x_ref[...]
    h = (_standardize(x) * (1.0 + mod_ref[1:2, :]) + mod_ref[0:1, :]).astype(BF16)
    p_xm = jnp.dot(h, wtok_ref[:, 0:d], preferred_element_type=F32)
    p_om = jnp.dot(h, wtok_ref[:, d:2 * d], preferred_element_type=F32)
    xm_ref[...] = (p_xm + btok_ref[:, 0:d]).astype(BF16)
    p_ga = jnp.dot(h, wtok_ref[:, 2 * d:3 * d], preferred_element_type=F32)
    som_ref[...] = _sigmoid(p_om + btok_ref[:, d:2 * d]).astype(BF16)
    pt_us = lax.dot_general(wt_ref[0:s5w, :], h, NT_DIMS, preferred_element_type=F32)
    sga_ref[...] = _sigmoid(p_ga + btok_ref[:, 2 * d:3 * d]).astype(BF16)
    pt_gb = lax.dot_general(wt_ref[s5w:s5w + d, :], h, NT_DIMS, preferred_element_type=F32)
    ust_ref[...] = (pt_us + bt_ref[0:s5w, :]).astype(BF16)
    ng = 2 * MLSTM_HEADS
    pt_g = lax.dot_general(wt_ref[s5w + d:s5w + d + ng, :], h, NT_DIMS, preferred_element_type=F32)
    sgbt_ref[...] = _sigmoid(pt_gb + bt_ref[s5w:s5w + d, :]).astype(BF16)
    gt_ref[...] = pt_g + bt_ref[s5w + d:s5w + d + ng, :]


def _inproj(x, mod, wtok, btok, wt, bt, *, tm, s5w):
    bsz, s, d = x.shape
    nt = wt.shape[0]
    ng = 2 * MLSTM_HEADS
    tok = lambda b, i: (b, i, 0)
    chan = lambda b, i: (b, 0, i)
    const = lambda b, i: (0, 0)
    return pl.pallas_call(
        functools.partial(_inproj_kernel, d=d, s5w=s5w),
        grid=(bsz, s // tm),
        in_specs=[pl.BlockSpec((None, tm, d), tok),
                  pl.BlockSpec((None, 6, d), lambda b, i: (b, 0, 0)),
                  pl.BlockSpec((d, 3 * d), const),
                  pl.BlockSpec((1, 3 * d), const),
                  pl.BlockSpec((nt, d), const),
                  pl.BlockSpec((nt, 1), const)],
        out_specs=[pl.BlockSpec((None, tm, d), tok),
                   pl.BlockSpec((None, tm, d), tok),
                   pl.BlockSpec((None, tm, d), tok),
                   pl.BlockSpec((None, s5w, tm), chan),
                   pl.BlockSpec((None, d, tm), chan),
                   pl.BlockSpec((None, ng, tm), chan)],
        out_shape=[jax.ShapeDtypeStruct((bsz, s, d), BF16),
                   jax.ShapeDtypeStruct((bsz, s, d), BF16),
                   jax.ShapeDtypeStruct((bsz, s, d), BF16),
                   jax.ShapeDtypeStruct((bsz, s5w, s), BF16),
                   jax.ShapeDtypeStruct((bsz, d, s), BF16),
                   jax.ShapeDtypeStruct((bsz, ng, s), F32)],
        compiler_params=_params("arbitrary", "arbitrary"),
        name="inproj",
    )(x, mod, wtok, btok, wt, bt)


def _lane_cumsum(x):
    n = x.shape[-1]
    lane = lax.broadcasted_iota(jnp.int32, x.shape, x.ndim - 1)
    sh = 1
    while sh < n:
        x = x + jnp.where(lane >= sh, pltpu.roll(x, sh, x.ndim - 1), 0.0)
        sh *= 2
    return x


def _mlstm_kernel(xm_ref, gt_ref, wconv_ref, bconv_ref, wq_ref, wk_ref,
                  out_ref, xbuf, c_ref, n_ref, m_ref, *, chunk, dv, dk):
    L = chunk
    H = MLSTM_HEADS

    @pl.when(pl.program_id(1) == 0)
    def _():
        xbuf[L:L + SUBLANES, :] = jnp.zeros((SUBLANES, xbuf.shape[1]), F32)
        c_ref[...] = jnp.zeros(c_ref.shape, F32)
        n_ref[...] = jnp.zeros(n_ref.shape, F32)
        m_ref[...] = jnp.zeros(m_ref.shape, F32)

    xbuf[0:SUBLANES, :] = xbuf[L:L + SUBLANES, :]
    xbuf[SUBLANES:SUBLANES + L, :] = xm_ref[...].astype(F32)

    ti = lax.broadcasted_iota(jnp.int32, (L, L), 0)
    si = lax.broadcasted_iota(jnp.int32, (L, L), 1)
    causal = si <= ti
    scale = dk ** -0.5

    c_old = [c_ref[h] for h in range(H)]
    n_old = [n_ref[h, 0:1, :] for h in range(H)]
    m_old = [m_ref[h, 0:1, 0:1] for h in range(H)]
    c_new, n_new, m_new_all = [], [], []

    heads = range(H)
    cols = [slice(h * dv, (h + 1) * dv) for h in heads]

    g = gt_ref[...]
    row = lax.broadcasted_iota(jnp.int32, g.shape, 0)
    bcum = _lane_cumsum(jnp.where(row >= H, _log_sigmoid(g), 0.0))
    r_all = jnp.where(row >= H, bcum, g)
    eye = (ti == si).astype(F32)
    c_all = lax.dot_general(eye, r_all, NT_DIMS, precision=HIGHEST, preferred_element_type=F32)
    ig_s = [r_all[h:h + 1, :] for h in heads]
    b_s = [r_all[H + h:H + h + 1, :] for h in heads]
    ig_t = [c_all[:, h:h + 1] for h in heads]
    b_t = [c_all[:, H + h:H + h + 1] for h in heads]
    b_last = [b_s[h][:, L - 1:L] for h in heads]

    xc, q, k, qb, kb, v = [], [], [], [], [], []
    for h in heads:
        acc = bconv_ref[:, cols[h]]
        for j in range(MLSTM_CONV):
            off = SUBLANES - (MLSTM_CONV - 1) + j
            acc = acc + wconv_ref[j:j + 1, cols[h]] * xbuf[off:off + L, cols[h]]
        xc.append((acc * _sigmoid(acc)).astype(BF16))
    for h in heads:
        q.append(jnp.dot(xc[h], wq_ref[h], preferred_element_type=F32) * scale)
        k.append(jnp.dot(xc[h], wk_ref[h], preferred_element_type=F32))
        qb.append(q[h].astype(BF16))
        kb.append(k[h].astype(BF16))
        v.append(xm_ref[:, cols[h]])

    m_t, sc_inter, wmat, s = [], [], [], []
    for h in heads:
        dmat = jnp.where(causal, b_t[h] - b_s[h] + ig_s[h], -jnp.inf)
        inter = b_t[h] + m_old[h]
        m_t.append(jnp.maximum(inter, jnp.max(dmat, axis=1, keepdims=True)))
        wmat.append(jnp.exp(dmat - m_t[h]))
        sc_inter.append(jnp.exp(inter - m_t[h]))
    for h in heads:
        s.append(lax.dot_general(qb[h], kb[h], NT_DIMS, preferred_element_type=F32) * wmat[h])

    for h in heads:
        g_s = b_last[h] - b_s[h] + ig_s[h]
        m_new = jnp.maximum(b_last[h] + m_old[h], jnp.max(g_s, axis=1, keepdims=True))
        wk_t = jnp.exp(b_last[h] - b_t[h] + ig_t[h] - m_new)
        decay = jnp.exp(b_last[h] + m_old[h] - m_new)
        kw = k[h] * wk_t
        c_new.append(decay * c_old[h]
                     + lax.dot_general(kw.astype(BF16), v[h], TN_DIMS, preferred_element_type=F32))
        n_new.append(decay * n_old[h] + jnp.sum(kw, axis=0, keepdims=True))
        m_new_all.append(m_new)

    for h in heads:
        num = (jnp.dot(s[h].astype(BF16), v[h], preferred_element_type=F32)
               + sc_inter[h] * jnp.dot(qb[h], c_old[h].astype(BF16), preferred_element_type=F32))
        den = (jnp.sum(s[h], axis=1, keepdims=True)
               + sc_inter[h] * jnp.sum(q[h] * n_old[h], axis=1, keepdims=True))
        out_ref[:, cols[h]] = (num / jnp.maximum(jnp.abs(den), jnp.exp(-m_t[h]))).astype(BF16)

    for h in range(H):
        c_ref[h] = c_new[h]
        n_ref[h, 0:1, :] = n_new[h]
        m_ref[h] = jnp.broadcast_to(m_new_all[h], m_ref.shape[1:])


def _mlstm(xm, gt, wconv, bconv, wq, wk, *, chunk):
    bsz, s, d = xm.shape
    H = MLSTM_HEADS
    dv = d // H
    dk = wq.shape[-1]
    ng = gt.shape[1]
    tok = lambda b, j: (b, j, 0)
    const2 = lambda b, j: (0, 0)
    const3 = lambda b, j: (0, 0, 0)
    return pl.pallas_call(
        functools.partial(_mlstm_kernel, chunk=chunk, dv=dv, dk=dk),
        grid=(bsz, s // chunk),
        in_specs=[pl.BlockSpec((None, chunk, d), tok),
                  pl.BlockSpec((None, ng, chunk), lambda b, j: (b, 0, j)),
                  pl.BlockSpec((MLSTM_CONV, d), const2),
                  pl.BlockSpec((1, d), const2),
                  pl.BlockSpec((H, dv, dk), const3),
                  pl.BlockSpec((H, dv, dk), const3)],
        out_specs=pl.BlockSpec((None, chunk, d), tok),
        out_shape=jax.ShapeDtypeStruct((bsz, s, d), BF16),
        scratch_shapes=[pltpu.VMEM((chunk + SUBLANES, d), F32),
                        pltpu.VMEM((H, dk, dv), F32),
                        pltpu.VMEM((H, SUBLANES, dk), F32),
                        pltpu.VMEM((H, SUBLANES, LANES), F32)],
        compiler_params=_params("arbitrary", "arbitrary"),
        name="mlstm",
    )(xm, gt, wconv, bconv, wq, wk)


def _s5_kernel(us_ref, ldt_ref, lamr_ref, lami_ref, lamc_ref, bt_ref, cc_ref, ca_ref, cb_ref, d_ref,
               out_ref, toep_ref, wb_ref, wc_ref, *, nb, nchunk):
    P = S5_STATE
    NC = S5_GROUP
    LS = LANES
    M = nb * nchunk

    dt = jnp.exp(ldt_ref[...])
    lam_re = lamr_ref[...]
    lam_im = lami_ref[...]
    lane2 = lax.broadcasted_iota(jnp.int32, (1, 2 * P), 1)
    sgn = jnp.where(lane2 < P, -1.0, 1.0)

    mag = jnp.exp(lam_re * dt)
    ar = mag * jnp.cos(lam_im * dt)
    ai = mag * jnp.sin(lam_im * dt)
    den = lam_re * lam_re + lam_im * lam_im
    zr = ((ar - 1.0) * lam_re + ai * lam_im) / den
    zi = (ai * lam_re - (ar - 1.0) * lam_im) / den
    b1 = bt_ref[...]
    bb = zr * b1 + zi * sgn * pltpu.roll(b1, P, 1)
    bb_sw = pltpu.roll(bb, P, 1)

    cr2 = cc_ref[0]
    ci2 = cc_ref[1]
    m12 = (cr2[:, None, :] * (bb * (-sgn))[None, :, :]
           - ci2[:, None, :] * bb_sw[None, :, :]).reshape(NC * NC, 2 * P)
    lam_re_c = lamc_ref[0]
    lam_im_c = lamc_ref[1]
    tau = lax.broadcasted_iota(jnp.int32, (2 * P, LS), 1).astype(F32)
    prow = lax.broadcasted_iota(jnp.int32, (2 * P, LS), 0)
    e0 = jnp.exp(lam_re_c * dt * tau)
    ang0 = lam_im_c * dt * tau
    pr0 = e0 * jnp.cos(ang0)
    pi0 = e0 * jnp.sin(ang0)
    pstack = jnp.where(prow < P, pr0, pi0)
    kpairs = jnp.dot(m12, pstack, precision=HIGHEST, preferred_element_type=F32)

    srow = lax.broadcasted_iota(jnp.int32, (LS, LS), 0)
    tcol = lax.broadcasted_iota(jnp.int32, (LS, LS), 1)
    lower = tcol >= srow
    srev = (LS - 1) - lax.broadcasted_iota(jnp.int32, (LS, 2 * P), 0).astype(F32)
    e1 = jnp.exp(lam_re * dt * srev)
    ang1 = lam_im * dt * srev
    r2 = e1 * jnp.cos(ang1)
    i2 = e1 * jnp.sin(ang1)
    bb_rot = bb_sw * sgn

    y = None
    x_end = None
    for cp0 in range(0, NC, S5_CBLOCK):
        for cp in range(cp0, cp0 + S5_CBLOCK):
            for c in range(NC):
                r = c * NC + cp
                kv = jnp.broadcast_to(kpairs[r:r + 1, :], (LS, LS))
                tz = pltpu.roll(kv, 0, 1, stride=1, stride_axis=0)
                toep_ref[cp * LS:(cp + 1) * LS, c * LS:(c + 1) * LS] = jnp.where(lower, tz, 0.0).astype(BF16)
            wb_ref[cp * LS:(cp + 1) * LS, :] = (r2 * bb[cp:cp + 1, :] + i2 * bb_rot[cp:cp + 1, :]).astype(BF16)
        rows = slice(cp0 * LS, (cp0 + S5_CBLOCK) * LS)
        ub = jnp.concatenate(
            [jnp.concatenate([us_ref[b, cp] for b in range(nb)], axis=0) for cp in range(cp0, cp0 + S5_CBLOCK)],
            axis=1)
        yp = jnp.dot(ub, toep_ref[rows, :], preferred_element_type=F32)
        xp = jnp.dot(ub, wb_ref[rows, :], preferred_element_type=F32)
        y = yp if y is None else y + yp
        x_end = xp if x_end is None else x_end + xp

    ar_c = jnp.exp(lam_re_c * dt) * jnp.cos(lam_im_c * dt)
    ai_c = jnp.exp(lam_re_c * dt) * jnp.sin(lam_im_c * dt)
    pr1 = pr0 * ar_c - pi0 * ai_c
    pi1 = pr0 * ai_c + pi0 * ar_c
    ca = ca_ref[...]
    cb = cb_ref[...]
    for c in range(NC):
        wc_ref[:, c * LS:(c + 1) * LS] = (ca[:, c:c + 1] * pr1 + cb[:, c:c + 1] * pi1).astype(BF16)

    jrow = lax.broadcasted_iota(jnp.int32, (M, 2 * P), 0) % nchunk
    lvl = lax.broadcasted_iota(jnp.int32, (SUBLANES, 2 * P), 0)
    nstep = (LS * jnp.left_shift(1, lvl)).astype(F32)
    el = jnp.exp(lam_re * dt * nstep)
    angl = lam_im * dt * nstep
    pl_all = el * jnp.cos(angl)
    ql_all = el * jnp.sin(angl) * sgn
    xs = x_end
    dstep = 1
    level = 0
    while dstep < nchunk:
        sh = jnp.where(jrow >= dstep, pltpu.roll(xs, dstep, 0), 0.0)
        xs = xs + pl_all[level:level + 1, :] * sh + ql_all[level:level + 1, :] * pltpu.roll(sh, P, 1)
        dstep *= 2
        level += 1
    x_prev = jnp.where(jrow >= 1, pltpu.roll(xs, 1, 0), 0.0)
    y = y + jnp.dot(x_prev.astype(BF16), wc_ref[...], preferred_element_type=F32)

    for c in range(NC):
        yc = y[:, c * LS:(c + 1) * LS]
        for b in range(nb):
            rows = slice(b * nchunk, (b + 1) * nchunk)
            yy = yc[rows, :] + d_ref[c:c + 1, 0:1] * us_ref[b, c].astype(F32)
            out_ref[b, c] = _gelu_tanh(yy).astype(BF16)


def _s5(ust4, ldt, lamr2, lami2, lamc, bt2, cc2, ca, cb, dcol):
    nb, w, nchunk, ls = ust4.shape
    G = w // S5_GROUP
    P2 = 2 * S5_STATE
    NC = S5_GROUP
    M = nb * nchunk
    g3 = lambda g: (g, 0, 0)
    g4 = lambda g: (g, 0, 0, 0)
    return pl.pallas_call(
        functools.partial(_s5_kernel, nb=nb, nchunk=nchunk),
        grid=(G,),
        in_specs=[pl.BlockSpec((nb, NC, nchunk, ls), lambda g: (0, g, 0, 0)),
                  pl.BlockSpec((None, 1, 1), g3),
                  pl.BlockSpec((None, 1, P2), g3),
                  pl.BlockSpec((None, 1, P2), g3),
                  pl.BlockSpec((None, 2, P2, 1), g4),
                  pl.BlockSpec((None, NC, P2), g3),
                  pl.BlockSpec((None, 2, NC, P2), g4),
                  pl.BlockSpec((None, P2, NC), g3),
                  pl.BlockSpec((None, P2, NC), g3),
                  pl.BlockSpec((None, NC, 1), g3)],
        out_specs=pl.BlockSpec((nb, NC, nchunk, ls), lambda g: (0, g, 0, 0)),
        out_shape=jax.ShapeDtypeStruct(ust4.shape, BF16),
        scratch_shapes=[pltpu.VMEM((NC * ls, NC * ls), BF16),
                        pltpu.VMEM((NC * ls, P2), BF16),
                        pltpu.VMEM((P2, NC * ls), BF16)],
        compiler_params=_params("arbitrary"),
        name="s5",
    )(ust4, ldt, lamr2, lami2, lamc, bt2, cc2, ca, cb, dcol)


def _tail_kernel(x_ref, mod_ref, hh_ref, som_ref, gain_ref, sga_ref, yst_ref, sgbt_ref, wdown_ref, wglut_ref,
                 wout_ref, g1_ref, b1_ref, x1_ref, h2_ref, *, d, alpha):
    dv = d // MLSTM_HEADS
    tm = x_ref.shape[0]
    sub = tm // TAIL_SUBTILES
    rows = [slice(i * sub, (i + 1) * sub) for i in range(TAIL_SUBTILES)]
    vgt = [jnp.dot(wglut_ref[...], yst_ref[:, r], preferred_element_type=F32) for r in rows]
    ya = [jnp.concatenate(
        [(_standardize(hh_ref[r, h * dv:(h + 1) * dv].astype(F32)) * gain_ref[:, h * dv:(h + 1) * dv]
          * som_ref[r, h * dv:(h + 1) * dv].astype(F32)).astype(BF16) for h in range(MLSTM_HEADS)], axis=1)
        for r in rows]
    y_a = [jnp.dot(ya[i], wdown_ref[...], preferred_element_type=F32) for i in range(TAIL_SUBTILES)]
    z = []
    for i, r in enumerate(rows):
        zbt = vgt[i][0:d, :] * _sigmoid(vgt[i][d:2 * d, :]) * sgbt_ref[:, r].astype(F32)
        z.append((sga_ref[r, :].astype(F32) * y_a[i] + zbt.T).astype(BF16))
    mix = [jnp.dot(z[i], wout_ref[...], preferred_element_type=F32) for i in range(TAIL_SUBTILES)]
    for i, r in enumerate(rows):
        res = alpha * x_ref[r, :] + (1.0 + mod_ref[2:3, :]) * mix[i]
        x1 = _standardize(res) * g1_ref[...] + b1_ref[...]
        x1_ref[r, :] = x1
        h2_ref[r, :] = (_standardize(x1) * (1.0 + mod_ref[4:5, :]) + mod_ref[3:4, :]).astype(BF16)


def _tail(x, mod, hh, som, gain, sga, yst, sgbt, wdown, wglut, wout, g1, b1, *, tm, alpha):
    bsz, s, d = x.shape
    s5w = yst.shape[1]
    tok = lambda b, i: (b, i, 0)
    chan = lambda b, i: (b, 0, i)
    const = lambda b, i: (0, 0)
    return pl.pallas_call(
        functools.partial(_tail_kernel, d=d, alpha=alpha),
        grid=(bsz, s // tm),
        in_specs=[pl.BlockSpec((None, tm, d), tok),
                  pl.BlockSpec((None, 6, d), lambda b, i: (b, 0, 0)),
                  pl.BlockSpec((None, tm, d), tok),
                  pl.BlockSpec((None, tm, d), tok),
                  pl.BlockSpec((1, d), const),
                  pl.BlockSpec((None, tm, d), tok),
                  pl.BlockSpec((None, s5w, tm), chan),
                  pl.BlockSpec((None, d, tm), chan),
                  pl.BlockSpec((d, d), const),
                  pl.BlockSpec((2 * d, s5w), const),
                  pl.BlockSpec((d, d), const),
                  pl.BlockSpec((1, d), const),
                  pl.BlockSpec((1, d), const)],
        out_specs=[pl.BlockSpec((None, tm, d), tok),
                   pl.BlockSpec((None, tm, d), tok)],
        out_shape=[jax.ShapeDtypeStruct((bsz, s, d), F32),
                   jax.ShapeDtypeStruct((bsz, s, d), BF16)],
        compiler_params=_params("arbitrary", "arbitrary"),
        name="tail",
    )(x, mod, hh, som, gain, sga, yst, sgbt, wdown, wglut, wout, g1, b1)


def _ffn_kernel(x1_ref, h2_ref, mod_ref, wup_ref, wconv_ref, bconv_ref, wdown_ref, g2_ref, b2_ref,
                out_ref, gbuf, *, hidden, alpha):
    tm = h2_ref.shape[0]

    @pl.when(pl.program_id(1) == 0)
    def _():
        gbuf[tm:tm + SUBLANES, :] = jnp.zeros((SUBLANES, hidden), F32)

    sub = tm // FFN_SUBTILES
    rows = [slice(i * sub, (i + 1) * sub) for i in range(FFN_SUBTILES)]
    gbuf[0:SUBLANES, :] = gbuf[tm:tm + SUBLANES, :]
    gate = []
    for i, r in enumerate(rows):
        gate.append(jnp.dot(h2_ref[r, :], wup_ref[:, hidden:2 * hidden], preferred_element_type=F32))
        gbuf[SUBLANES + i * sub:SUBLANES + (i + 1) * sub, :] = gate[i]
    act = []
    for i, r in enumerate(rows):
        conv = bconv_ref[...] + wconv_ref[FFN_CONV - 1:FFN_CONV, :] * gate[i]
        for k in range(1, FFN_CONV):
            off = SUBLANES + i * sub - k
            conv = conv + wconv_ref[FFN_CONV - 1 - k:FFN_CONV - k, :] * gbuf[off:off + sub, :]
        gact = _gelu_tanh(conv)
        val = jnp.dot(h2_ref[r, :], wup_ref[:, 0:hidden], preferred_element_type=F32)
        act.append((gact * val).astype(BF16))
    down = [jnp.dot(act[i], wdown_ref[...], preferred_element_type=F32) for i in range(FFN_SUBTILES)]
    for i, r in enumerate(rows):
        res = alpha * x1_ref[r, :] + (1.0 + mod_ref[5:6, :]) * down[i]
        out_ref[r, :] = _standardize(res) * g2_ref[...] + b2_ref[...]


def _ffn(x1, h2, mod, wup, wconv, bconv, wdown, g2, b2, *, tm, alpha):
    bsz, s, d = x1.shape
    hidden = wdown.shape[0]
    tok = lambda b, i: (b, i, 0)
    const = lambda b, i: (0, 0)
    return pl.pallas_call(
        functools.partial(_ffn_kernel, hidden=hidden, alpha=alpha),
        grid=(bsz, s // tm),
        in_specs=[pl.BlockSpec((None, tm, d), tok),
                  pl.BlockSpec((None, tm, d), tok),
                  pl.BlockSpec((None, 6, d), lambda b, i: (b, 0, 0)),
                  pl.BlockSpec((d, 2 * hidden), const),
                  pl.BlockSpec((FFN_CONV, hidden), const),
                  pl.BlockSpec((1, hidden), const),
                  pl.BlockSpec((hidden, d), const),
                  pl.BlockSpec((1, d), const),
                  pl.BlockSpec((1, d), const)],
        out_specs=pl.BlockSpec((None, tm, d), tok),
        out_shape=jax.ShapeDtypeStruct((bsz, s, d), F32),
        scratch_shapes=[pltpu.VMEM((tm + SUBLANES, hidden), F32)],
        compiler_params=_params("arbitrary", "arbitrary"),
        name="ffn",
    )(x1, h2, mod, wup, wconv, bconv, wdown, g2, b2)


def _layer(x, mod, w_in, b_in, w_mlstm_conv, b_mlstm_conv, w_mlstm_q, w_mlstm_k, mlstm_norm_gain,
           w_mlstm_down, s5_lam_re, s5_lam_im, s5_log_dt, s5_b_re, s5_b_im, s5_c_re, s5_c_im, s5_d,
           w_s5_glu, w_mix_out, ln1_gain, ln1_bias, w_ffn_up, w_ffn_conv, b_ffn_conv, w_ffn_down,
           ln2_gain, ln2_bias, *, alpha):
    bsz, s, d = x.shape
    H = MLSTM_HEADS
    s5w = s5_d.shape[0]
    G = s5w // S5_GROUP
    tm = min(512, s)
    chunk = min(256, s)

    o_om, o_ip, o_fp, o_us = d, 2 * d, 2 * d + H, 2 * d + 2 * H
    o_ga, o_gb = o_us + s5w, o_us + s5w + d
    tok_cols = (slice(0, 2 * d), slice(o_ga, o_ga + d))
    chan_cols = (slice(o_us, o_us + s5w), slice(o_gb, o_gb + d), slice(o_ip, o_ip + 2 * H))
    wtok = jnp.concatenate([w_in[:, c].astype(BF16) for c in tok_cols], axis=1)
    btok = jnp.concatenate([b_in[c] for c in tok_cols]).reshape(1, -1)
    wt = jnp.concatenate([w_in[:, c].T.astype(BF16) for c in chan_cols], axis=0)
    bt = jnp.concatenate([b_in[c] for c in chan_cols]).reshape(-1, 1)

    xm, som, sga, ust, sgbt, gt = _inproj(x, mod, wtok, btok, wt, bt, tm=min(1024, s), s5w=s5w)

    hh = _mlstm(xm, gt, w_mlstm_conv, b_mlstm_conv.reshape(1, d), w_mlstm_q.astype(BF16),
                w_mlstm_k.astype(BF16), chunk=chunk)

    dup = lambda a: jnp.concatenate([a, a], axis=-1)
    lamr2 = dup(s5_lam_re)[:, None, :]
    lami2 = dup(s5_lam_im)[:, None, :]
    lamc = jnp.stack([dup(s5_lam_re), dup(s5_lam_im)], axis=1)[..., None]
    bt2 = jnp.concatenate([jnp.swapaxes(s5_b_re, 1, 2), jnp.swapaxes(s5_b_im, 1, 2)], axis=-1)
    cc2 = jnp.stack([dup(s5_c_re), dup(s5_c_im)], axis=1)
    crt = jnp.swapaxes(s5_c_re, 1, 2)
    cit = jnp.swapaxes(s5_c_im, 1, 2)
    ca = jnp.concatenate([crt, -cit], axis=1)
    cb = jnp.concatenate([-cit, -crt], axis=1)
    yst = _s5(ust.reshape(bsz, s5w, s // LANES, LANES), s5_log_dt.reshape(G, 1, 1), lamr2, lami2, lamc,
              bt2, cc2, ca, cb, s5_d.reshape(G, S5_GROUP, 1)).reshape(bsz, s5w, s)

    x1, h2 = _tail(x, mod, hh, som, mlstm_norm_gain.reshape(1, d), sga, yst, sgbt,
                   w_mlstm_down.astype(BF16), w_s5_glu.T.astype(BF16),
                   w_mix_out.astype(BF16), ln1_gain.reshape(1, d), ln1_bias.reshape(1, d), tm=tm, alpha=alpha)

    hidden = w_ffn_down.shape[0]
    return _ffn(x1, h2, mod, w_ffn_up.astype(BF16), w_ffn_conv, b_ffn_conv.reshape(1, hidden),
                w_ffn_down.astype(BF16), ln2_gain.reshape(1, d), ln2_bias.reshape(1, d),
                tm=tm, alpha=alpha)


def kernel(x, c, w_ada, b_ada, w_in, b_in, w_mlstm_conv, b_mlstm_conv, w_mlstm_q, w_mlstm_k, mlstm_norm_gain, w_mlstm_down, s5_lam_re, s5_lam_im, s5_log_dt, s5_b_re, s5_b_im, s5_c_re, s5_c_im, s5_d, w_s5_glu, w_mix_out, ln1_gain, ln1_bias, w_ffn_up, w_ffn_conv, b_ffn_conv, w_ffn_down, ln2_gain, ln2_bias):
    depth = w_ada.shape[0]
    alpha = (2.0 * depth) ** 0.25
    bsz, d = c.shape
    for l in range(depth):
        mod = _adaln(c, w_ada[l], b_ada[l]).reshape(bsz, 6, d)
        x = _layer(x, mod, w_in[l], b_in[l], w_mlstm_conv[l], b_mlstm_conv[l], w_mlstm_q[l], w_mlstm_k[l],
                   mlstm_norm_gain[l], w_mlstm_down[l], s5_lam_re[l], s5_lam_im[l], s5_log_dt[l], s5_b_re[l],
                   s5_b_im[l], s5_c_re[l], s5_c_im[l], s5_d[l], w_s5_glu[l], w_mix_out[l], ln1_gain[l],
                   ln1_bias[l], w_ffn_up[l], w_ffn_conv[l], b_ffn_conv[l], w_ffn_down[l], ln2_gain[l],
                   ln2_bias[l], alpha=alpha)
    return x
```

```python
import functools
import math

import jax
import jax.numpy as jnp
from jax import lax
from jax.experimental import pallas as pl
from jax.experimental.pallas import tpu as pltpu

F32 = jnp.float32
BF16 = jnp.bfloat16
HIGHEST = lax.Precision.HIGHEST

LN_EPS = 1e-5
MLSTM_HEADS = 4
MLSTM_CONV = 4
FFN_CONV = 3
S5_GROUP = 16
S5_STATE = 64
S5_CBLOCK = 2
TAIL_SUBTILES = 2
FFN_SUBTILES = 2
LANES = 128
SUBLANES = 8
VMEM_LIMIT_BYTES = 56 * 1024 * 1024

NT_DIMS = (((1,), (1,)), ((), ()))
TN_DIMS = (((0,), (0,)), ((), ()))


def _standardize(x):
    mu = jnp.mean(x, axis=-1, keepdims=True)
    xc = x - mu
    var = jnp.mean(xc * xc, axis=-1, keepdims=True)
    return xc * lax.rsqrt(var + LN_EPS)


def _sigmoid(x):
    return 1.0 / (1.0 + jnp.exp(-x))


def _gelu_tanh(x):
    return 0.5 * x * (1.0 + jnp.tanh(math.sqrt(2.0 / math.pi) * (x + 0.044715 * (x * x * x))))


def _params(*semantics):
    return pltpu.CompilerParams(dimension_semantics=semantics, vmem_limit_bytes=VMEM_LIMIT_BYTES)


def _adaln_kernel(c_ref, w_ref, b_ref, o_ref):
    c = c_ref[...]
    ca = c * _sigmoid(c)
    o_ref[...] = jnp.dot(ca, w_ref[...], precision=HIGHEST, preferred_element_type=F32) + b_ref[...]


def _adaln(c, w, b):
    bsz, d = c.shape
    n = w.shape[1]
    return pl.pallas_call(
        _adaln_kernel,
        grid=(n // d,),
        in_specs=[pl.BlockSpec((bsz, d), lambda j: (0, 0)),
                  pl.BlockSpec((d, d), lambda j: (0, j)),
                  pl.BlockSpec((1, d), lambda j: (0, j))],
        out_specs=pl.BlockSpec((bsz, d), lambda j: (0, j)),
        out_shape=jax.ShapeDtypeStruct((bsz, n), F32),
        compiler_params=_params("arbitrary"),
        name="adaln",
    )(c, w, b.reshape(1, n))


def _log_sigmoid(g):
    return jnp.minimum(g, 0.0) - jnp.log(1.0 + jnp.exp(-jnp.abs(g)))


def _inproj_kernel(x_ref, mod_ref, wtok_ref, btok_ref, wt_ref, bt_ref,
                   xm_ref, som_ref, sga_ref, ust_ref, sgbt_ref, gt_ref, *, d, s5w):
    x = x_ref[...]
    h = (_standardize(x) * (1.0 + mod_ref[1:2, :]) + mod_ref[0:1, :]).astype(BF16)
    p_xm = jnp.dot(h, wtok_ref[:, 0:d], preferred_element_type=F32)
    p_om = jnp.dot(h, wtok_ref[:, d:2 * d], preferred_element_type=F32)
    xm_ref[...] = (p_xm + btok_ref[:, 0:d]).astype(BF16)
    p_ga = jnp.dot(h, wtok_ref[:, 2 * d:3 * d], preferred_element_type=F32)
    som_ref[...] = _sigmoid(p_om + btok_ref[:, d:2 * d]).astype(BF16)
    pt_us = lax.dot_general(wt_ref[0:s5w, :], h, NT_DIMS, preferred_element_type=F32)
    sga_ref[...] = _sigmoid(p_ga + btok_ref[:, 2 * d:3 * d]).astype(BF16)
    pt_gb = lax.dot_general(wt_ref[s5w:s5w + d, :], h, NT_DIMS, preferred_element_type=F32)
    ust_ref[...] = (pt_us + bt_ref[0:s5w, :]).astype(BF16)
    ng = 2 * MLSTM_HEADS
    pt_g = lax.dot_general(wt_ref[s5w + d:s5w + d + ng, :], h, NT_DIMS, preferred_element_type=F32)
    sgbt_ref[...] = _sigmoid(pt_gb + bt_ref[s5w:s5w + d, :]).astype(BF16)
    gt_ref[...] = pt_g + bt_ref[s5w + d:s5w + d + ng, :]


def _inproj(x, mod, wtok, btok, wt, bt, *, tm, s5w):
    bsz, s, d = x.shape
    nt = wt.shape[0]
    ng = 2 * MLSTM_HEADS
    tok = lambda b, i: (b, i, 0)
    chan = lambda b, i: (b, 0, i)
    const = lambda b, i: (0, 0)
    return pl.pallas_call(
        functools.partial(_inproj_kernel, d=d, s5w=s5w),
        grid=(bsz, s // tm),
        in_specs=[pl.BlockSpec((None, tm, d), tok),
                  pl.BlockSpec((None, 6, d), lambda b, i: (b, 0, 0)),
                  pl.BlockSpec((d, 3 * d), const),
                  pl.BlockSpec((1, 3 * d), const),
                  pl.BlockSpec((nt, d), const),
                  pl.BlockSpec((nt, 1), const)],
        out_specs=[pl.BlockSpec((None, tm, d), tok),
                   pl.BlockSpec((None, tm, d), tok),
                   pl.BlockSpec((None, tm, d), tok),
                   pl.BlockSpec((None, s5w, tm), chan),
                   pl.BlockSpec((None, d, tm), chan),
                   pl.BlockSpec((None, ng, tm), chan)],
        out_shape=[jax.ShapeDtypeStruct((bsz, s, d), BF16),
                   jax.ShapeDtypeStruct((bsz, s, d), BF16),
                   jax.ShapeDtypeStruct((bsz, s, d), BF16),
                   jax.ShapeDtypeStruct((bsz, s5w, s), BF16),
                   jax.ShapeDtypeStruct((bsz, d, s), BF16),
                   jax.ShapeDtypeStruct((bsz, ng, s), F32)],
        compiler_params=_params("arbitrary", "arbitrary"),
        name="inproj",
    )(x, mod, wtok, btok, wt, bt)


def _lane_cumsum(x):
    n = x.shape[-1]
    lane = lax.broadcasted_iota(jnp.int32, x.shape, x.ndim - 1)
    sh = 1
    while sh < n:
        x = x + jnp.where(lane >= sh, pltpu.roll(x, sh, x.ndim - 1), 0.0)
        sh *= 2
    return x


def _mlstm_kernel(xm_ref, gt_ref, wconv_ref, bconv_ref, wq_ref, wk_ref,
                  out_ref, xbuf, c_ref, n_ref, m_ref, *, chunk, dv, dk):
    L = chunk
    H = MLSTM_HEADS

    @pl.when(pl.program_id(1) == 0)
    def _():
        xbuf[L:L + SUBLANES, :] = jnp.zeros((SUBLANES, xbuf.shape[1]), F32)
        c_ref[...] = jnp.zeros(c_ref.shape, F32)
        n_ref[...] = jnp.zeros(n_ref.shape, F32)
        m_ref[...] = jnp.zeros(m_ref.shape, F32)

    xbuf[0:SUBLANES, :] = xbuf[L:L + SUBLANES, :]
    xbuf[SUBLANES:SUBLANES + L, :] = xm_ref[...].astype(F32)

    ti = lax.broadcasted_iota(jnp.int32, (L, L), 0)
    si = lax.broadcasted_iota(jnp.int32, (L, L), 1)
    causal = si <= ti
    scale = dk ** -0.5

    c_old = [c_ref[h] for h in range(H)]
    n_old = [n_ref[h, 0:1, :] for h in range(H)]
    m_old = [m_ref[h, 0:1, 0:1] for h in range(H)]
    c_new, n_new, m_new_all = [], [], []

    heads = range(H)
    cols = [slice(h * dv, (h + 1) * dv) for h in heads]

    g = gt_ref[...]
    row = lax.broadcasted_iota(jnp.int32, g.shape, 0)
    bcum = _lane_cumsum(jnp.where(row >= H, _log_sigmoid(g), 0.0))
    r_all = jnp.where(row >= H, bcum, g)
    eye = (ti == si).astype(F32)
    c_all = lax.dot_general(eye, r_all, NT_DIMS, precision=HIGHEST, preferred_element_type=F32)
    ig_s = [r_all[h:h + 1, :] for h in heads]
    b_s = [r_all[H + h:H + h + 1, :] for h in heads]
    ig_t = [c_all[:, h:h + 1] for h in heads]
    b_t = [c_all[:, H + h:H + h + 1] for h in heads]
    b_last = [b_s[h][:, L - 1:L] for h in heads]

    xc, q, k, qb, kb, v = [], [], [], [], [], []
    for h in heads:
        acc = bconv_ref[:, cols[h]]
        for j in range(MLSTM_CONV):
            off = SUBLANES - (MLSTM_CONV - 1) + j
            acc = acc + wconv_ref[j:j + 1, cols[h]] * xbuf[off:off + L, cols[h]]
        xc.append((acc * _sigmoid(acc)).astype(BF16))
    for h in heads:
        q.append(jnp.dot(xc[h], wq_ref[h], preferred_element_type=F32) * scale)
        k.append(jnp.dot(xc[h], wk_ref[h], preferred_element_type=F32))
        qb.append(q[h].astype(BF16))
        kb.append(k[h].astype(BF16))
        v.append(xm_ref[:, cols[h]])

    m_t, sc_inter, wmat, s = [], [], [], []
    for h in heads:
        dmat = jnp.where(causal, b_t[h] - b_s[h] + ig_s[h], -jnp.inf)
        inter = b_t[h] + m_old[h]
        m_t.append(jnp.maximum(inter, jnp.max(dmat, axis=1, keepdims=True)))
        wmat.append(jnp.exp(dmat - m_t[h]))
        sc_inter.append(jnp.exp(inter - m_t[h]))
    for h in heads:
        s.append(lax.dot_general(qb[h], kb[h], NT_DIMS, preferred_element_type=F32) * wmat[h])

    for h in heads:
        g_s = b_last[h] - b_s[h] + ig_s[h]
        m_new = jnp.maximum(b_last[h] + m_old[h], jnp.max(g_s, axis=1, keepdims=True))
        wk_t = jnp.exp(b_last[h] - b_t[h] + ig_t[h] - m_new)
        decay = jnp.exp(b_last[h] + m_old[h] - m_new)
        kw = k[h] * wk_t
        c_new.append(decay * c_old[h]
                     + lax.dot_general(kw.astype(BF16), v[h], TN_DIMS, preferred_element_type=F32))
        n_new.append(decay * n_old[h] + jnp.sum(kw, axis=0, keepdims=True))
        m_new_all.append(m_new)

    for h in heads:
        num = (jnp.dot(s[h].astype(BF16), v[h], preferred_element_type=F32)
               + sc_inter[h] * jnp.dot(qb[h], c_old[h].astype(BF16), preferred_element_type=F32))
        den = (jnp.sum(s[h], axis=1, keepdims=True)
               + sc_inter[h] * jnp.sum(q[h] * n_old[h], axis=1, keepdims=True))
        out_ref[:, cols[h]] = (num / jnp.maximum(jnp.abs(den), jnp.exp(-m_t[h]))).astype(BF16)

    for h in range(H):
        c_ref[h] = c_new[h]
        n_ref[h, 0:1, :] = n_new[h]
        m_ref[h] = jnp.broadcast_to(m_new_all[h], m_ref.shape[1:])


def _mlstm(xm, gt, wconv, bconv, wq, wk, *, chunk):
    bsz, s, d = xm.shape
    H = MLSTM_HEADS
    dv = d // H
    dk = wq.shape[-1]
    ng = gt.shape[1]
    tok = lambda b, j: (b, j, 0)
    const2 = lambda b, j: (0, 0)
    const3 = lambda b, j: (0, 0, 0)
    return pl.pallas_call(
        functools.partial(_mlstm_kernel, chunk=chunk, dv=dv, dk=dk),
        grid=(bsz, s // chunk),
        in_specs=[pl.BlockSpec((None, chunk, d), tok),
                  pl.BlockSpec((None, ng, chunk), lambda b, j: (b, 0, j)),
                  pl.BlockSpec((MLSTM_CONV, d), const2),
                  pl.BlockSpec((1, d), const2),
                  pl.BlockSpec((H, dv, dk), const3),
                  pl.BlockSpec((H, dv, dk), const3)],
        out_specs=pl.BlockSpec((None, chunk, d), tok),
        out_shape=jax.ShapeDtypeStruct((bsz, s, d), BF16),
        scratch_shapes=[pltpu.VMEM((chunk + SUBLANES, d), F32),
                        pltpu.VMEM((H, dk, dv), F32),
                        pltpu.VMEM((H, SUBLANES, dk), F32),
                        pltpu.VMEM((H, SUBLANES, LANES), F32)],
        compiler_params=_params("arbitrary", "arbitrary"),
        name="mlstm",
    )(xm, gt, wconv, bconv, wq, wk)


def _s5_kernel(us_ref, pa_ref, pb_ref, pc_ref, out_ref, toep_ref, wb_ref, wc_ref, *, nb, nchunk):
    P = S5_STATE
    NC = S5_GROUP
    LS = LANES
    M = nb * nchunk

    dt = jnp.exp(pa_ref[2:3, 0:1])
    lam_re = pa_ref[0:1, :]
    lam_im = pa_ref[1:2, :]
    lane2 = lax.broadcasted_iota(jnp.int32, (1, 2 * P), 1)
    sgn = jnp.where(lane2 < P, -1.0, 1.0)

    mag = jnp.exp(lam_re * dt)
    ar = mag * jnp.cos(lam_im * dt)
    ai = mag * jnp.sin(lam_im * dt)
    den = lam_re * lam_re + lam_im * lam_im
    zr = ((ar - 1.0) * lam_re + ai * lam_im) / den
    zi = (ai * lam_re - (ar - 1.0) * lam_im) / den
    b1 = pb_ref[0:NC, :]
    bb = zr * b1 + zi * sgn * pltpu.roll(b1, P, 1)
    bb_sw = pltpu.roll(bb, P, 1)

    cr2 = pb_ref[NC:2 * NC, :]
    ci2 = pb_ref[2 * NC:3 * NC, :]
    m12 = (cr2[:, None, :] * (bb * (-sgn))[None, :, :]
           - ci2[:, None, :] * bb_sw[None, :, :]).reshape(NC * NC, 2 * P)
    lam_re_c = pc_ref[:, 2 * NC:2 * NC + 1]
    lam_im_c = pc_ref[:, 2 * NC + 1:2 * NC + 2]
    tau = lax.broadcasted_iota(jnp.int32, (2 * P, LS), 1).astype(F32)
    prow = lax.broadcasted_iota(jnp.int32, (2 * P, LS), 0)
    e0 = jnp.exp(lam_re_c * dt * tau)
    ang0 = lam_im_c * dt * tau
    pr0 = e0 * jnp.cos(ang0)
    pi0 = e0 * jnp.sin(ang0)
    pstack = jnp.where(prow < P, pr0, pi0)
    kpairs = jnp.dot(m12, pstack, precision=HIGHEST, preferred_element_type=F32)

    srow = lax.broadcasted_iota(jnp.int32, (LS, LS), 0)
    tcol = lax.broadcasted_iota(jnp.int32, (LS, LS), 1)
    lower = tcol >= srow
    srev = (LS - 1) - lax.broadcasted_iota(jnp.int32, (LS, 2 * P), 0).astype(F32)
    e1 = jnp.exp(lam_re * dt * srev)
    ang1 = lam_im * dt * srev
    r2 = e1 * jnp.cos(ang1)
    i2 = e1 * jnp.sin(ang1)
    bb_rot = bb_sw * sgn

    y = None
    x_end = None
    for cp0 in range(0, NC, S5_CBLOCK):
        for cp in range(cp0, cp0 + S5_CBLOCK):
            for c in range(NC):
                r = c * NC + cp
                kv = jnp.broadcast_to(kpairs[r:r + 1, :], (LS, LS))
                tz = pltpu.roll(kv, 0, 1, stride=1, stride_axis=0)
                toep_ref[cp * LS:(cp + 1) * LS, c * LS:(c + 1) * LS] = jnp.where(lower, tz, 0.0).astype(BF16)
            wb_ref[cp * LS:(cp + 1) * LS, :] = (r2 * bb[cp:cp + 1, :] + i2 * bb_rot[cp:cp + 1, :]).astype(BF16)
        rows = slice(cp0 * LS, (cp0 + S5_CBLOCK) * LS)
        ub = jnp.concatenate(
            [jnp.concatenate([us_ref[b, cp] for b in range(nb)], axis=0) for cp in range(cp0, cp0 + S5_CBLOCK)],
            axis=1)
        yp = jnp.dot(ub, toep_ref[rows, :], preferred_element_type=F32)
        xp = jnp.dot(ub, wb_ref[rows, :], preferred_element_type=F32)
        y = yp if y is None else y + yp
        x_end = xp if x_end is None else x_end + xp

    ar_c = jnp.exp(lam_re_c * dt) * jnp.cos(lam_im_c * dt)
    ai_c = jnp.exp(lam_re_c * dt) * jnp.sin(lam_im_c * dt)
    pr1 = pr0 * ar_c - pi0 * ai_c
    pi1 = pr0 * ai_c + pi0 * ar_c
    ca = pc_ref[:, 0:NC]
    cb = pc_ref[:, NC:2 * NC]
    for c in range(NC):
        wc_ref[:, c * LS:(c + 1) * LS] = (ca[:, c:c + 1] * pr1 + cb[:, c:c + 1] * pi1).astype(BF16)

    jrow = lax.broadcasted_iota(jnp.int32, (M, 2 * P), 0) % nchunk
    lvl = lax.broadcasted_iota(jnp.int32, (SUBLANES, 2 * P), 0)
    nstep = (LS * jnp.left_shift(1, lvl)).astype(F32)
    el = jnp.exp(lam_re * dt * nstep)
    angl = lam_im * dt * nstep
    pl_all = el * jnp.cos(angl)
    ql_all = el * jnp.sin(angl) * sgn
    xs = x_end
    dstep = 1
    level = 0
    while dstep < nchunk:
        sh = jnp.where(jrow >= dstep, pltpu.roll(xs, dstep, 0), 0.0)
        xs = xs + pl_all[level:level + 1, :] * sh + ql_all[level:level + 1, :] * pltpu.roll(sh, P, 1)
        dstep *= 2
        level += 1
    x_prev = jnp.where(jrow >= 1, pltpu.roll(xs, 1, 0), 0.0)
    y = y + jnp.dot(x_prev.astype(BF16), wc_ref[...], preferred_element_type=F32)

    for c in range(NC):
        yc = y[:, c * LS:(c + 1) * LS]
        for b in range(nb):
            rows = slice(b * nchunk, (b + 1) * nchunk)
            yy = yc[rows, :] + pc_ref[c:c + 1, 2 * NC + 2:2 * NC + 3] * us_ref[b, c].astype(F32)
            out_ref[b, c] = _gelu_tanh(yy).astype(BF16)


def _s5(ust4, pa, pb, pc):
    nb, w, nchunk, ls = ust4.shape
    G = w // S5_GROUP
    P2 = 2 * S5_STATE
    NC = S5_GROUP
    g3 = lambda g: (g, 0, 0)
    return pl.pallas_call(
        functools.partial(_s5_kernel, nb=nb, nchunk=nchunk),
        grid=(G,),
        in_specs=[pl.BlockSpec((nb, NC, nchunk, ls), lambda g: (0, g, 0, 0)),
                  pl.BlockSpec((None,) + pa.shape[1:], g3),
                  pl.BlockSpec((None,) + pb.shape[1:], g3),
                  pl.BlockSpec((None,) + pc.shape[1:], g3)],
        out_specs=pl.BlockSpec((nb, NC, nchunk, ls), lambda g: (0, g, 0, 0)),
        out_shape=jax.ShapeDtypeStruct(ust4.shape, BF16),
        scratch_shapes=[pltpu.VMEM((NC * ls, NC * ls), BF16),
                        pltpu.VMEM((NC * ls, P2), BF16),
                        pltpu.VMEM((P2, NC * ls), BF16)],
        compiler_params=_params("arbitrary"),
        name="s5",
    )(ust4, pa, pb, pc)


def _tail_kernel(x_ref, mod_ref, hh_ref, som_ref, gain_ref, sga_ref, yst_ref, sgbt_ref, wdown_ref, wglut_ref,
                 wout_ref, g1_ref, b1_ref, x1_ref, h2_ref, *, d, alpha):
    dv = d // MLSTM_HEADS
    tm = x_ref.shape[0]
    sub = tm // TAIL_SUBTILES
    rows = [slice(i * sub, (i + 1) * sub) for i in range(TAIL_SUBTILES)]
    vgt = [jnp.dot(wglut_ref[...], yst_ref[:, r], preferred_element_type=F32) for r in rows]
    ya = [jnp.concatenate(
        [(_standardize(hh_ref[r, h * dv:(h + 1) * dv].astype(F32)) * gain_ref[:, h * dv:(h + 1) * dv]
          * som_ref[r, h * dv:(h + 1) * dv].astype(F32)).astype(BF16) for h in range(MLSTM_HEADS)], axis=1)
        for r in rows]
    y_a = [jnp.dot(ya[i], wdown_ref[...], preferred_element_type=F32) for i in range(TAIL_SUBTILES)]
    z = []
    for i, r in enumerate(rows):
        zbt = vgt[i][0:d, :] * _sigmoid(vgt[i][d:2 * d, :]) * sgbt_ref[:, r].astype(F32)
        z.append((sga_ref[r, :].astype(F32) * y_a[i] + zbt.T).astype(BF16))
    mix = [jnp.dot(z[i], wout_ref[...], preferred_element_type=F32) for i in range(TAIL_SUBTILES)]
    for i, r in enumerate(rows):
        res = alpha * x_ref[r, :] + (1.0 + mod_ref[2:3, :]) * mix[i]
        x1 = _standardize(res) * g1_ref[...] + b1_ref[...]
        x1_ref[r, :] = x1
        h2_ref[r, :] = (_standardize(x1) * (1.0 + mod_ref[4:5, :]) + mod_ref[3:4, :]).astype(BF16)


def _tail(x, mod, hh, som, gain, sga, yst, sgbt, wdown, wglut, wout, g1, b1, *, tm, alpha):
    bsz, s, d = x.shape
    s5w = yst.shape[1]
    tok = lambda b, i: (b, i, 0)
    chan = lambda b, i: (b, 0, i)
    const = lambda b, i: (0, 0)
    return pl.pallas_call(
        functools.partial(_tail_kernel, d=d, alpha=alpha),
        grid=(bsz, s // tm),
        in_specs=[pl.BlockSpec((None, tm, d), tok),
                  pl.BlockSpec((None, 6, d), lambda b, i: (b, 0, 0)),
                  pl.BlockSpec((None, tm, d), tok),
                  pl.BlockSpec((None, tm, d), tok),
                  pl.BlockSpec((1, d), const),
                  pl.BlockSpec((None, tm, d), tok),
                  pl.BlockSpec((None, s5w, tm), chan),
                  pl.BlockSpec((None, d, tm), chan),
                  pl.BlockSpec((d, d), const),
                  pl.BlockSpec((2 * d, s5w), const),
                  pl.BlockSpec((d, d), const),
                  pl.BlockSpec((1, d), const),
                  pl.BlockSpec((1, d), const)],
        out_specs=[pl.BlockSpec((None, tm, d), tok),
                   pl.BlockSpec((None, tm, d), tok)],
        out_shape=[jax.ShapeDtypeStruct((bsz, s, d), F32),
                   jax.ShapeDtypeStruct((bsz, s, d), BF16)],
        compiler_params=_params("arbitrary", "arbitrary"),
        name="tail",
    )(x, mod, hh, som, gain, sga, yst, sgbt, wdown, wglut, wout, g1, b1)


def _ffn_kernel(x1_ref, h2_ref, mod_ref, wup_ref, wconv_ref, bconv_ref, wdown_ref, g2_ref, b2_ref,
                out_ref, gbuf, *, hidden, alpha):
    tm = h2_ref.shape[0]

    @pl.when(pl.program_id(1) == 0)
    def _():
        gbuf[tm:tm + SUBLANES, :] = jnp.zeros((SUBLANES, hidden), F32)

    sub = tm // FFN_SUBTILES
    rows = [slice(i * sub, (i + 1) * sub) for i in range(FFN_SUBTILES)]
    gbuf[0:SUBLANES, :] = gbuf[tm:tm + SUBLANES, :]
    gate = []
    for i, r in enumerate(rows):
        gate.append(jnp.dot(h2_ref[r, :], wup_ref[:, hidden:2 * hidden], preferred_element_type=F32))
        gbuf[SUBLANES + i * sub:SUBLANES + (i + 1) * sub, :] = gate[i]
    act = []
    for i, r in enumerate(rows):
        conv = bconv_ref[...] + wconv_ref[FFN_CONV - 1:FFN_CONV, :] * gate[i]
        for k in range(1, FFN_CONV):
            off = SUBLANES + i * sub - k
            conv = conv + wconv_ref[FFN_CONV - 1 - k:FFN_CONV - k, :] * gbuf[off:off + sub, :]
        gact = _gelu_tanh(conv)
        val = jnp.dot(h2_ref[r, :], wup_ref[:, 0:hidden], preferred_element_type=F32)
        act.append((gact * val).astype(BF16))
    down = [jnp.dot(act[i], wdown_ref[...], preferred_element_type=F32) for i in range(FFN_SUBTILES)]
    for i, r in enumerate(rows):
        res = alpha * x1_ref[r, :] + (1.0 + mod_ref[5:6, :]) * down[i]
        out_ref[r, :] = _standardize(res) * g2_ref[...] + b2_ref[...]


def _ffn(x1, h2, mod, wup, wconv, bconv, wdown, g2, b2, *, tm, alpha):
    bsz, s, d = x1.shape
    hidden = wdown.shape[0]
    tok = lambda b, i: (b, i, 0)
    const = lambda b, i: (0, 0)
    return pl.pallas_call(
        functools.partial(_ffn_kernel, hidden=hidden, alpha=alpha),
        grid=(bsz, s // tm),
        in_specs=[pl.BlockSpec((None, tm, d), tok),
                  pl.BlockSpec((None, tm, d), tok),
                  pl.BlockSpec((None, 6, d), lambda b, i: (b, 0, 0)),
                  pl.BlockSpec((d, 2 * hidden), const),
                  pl.BlockSpec((FFN_CONV, hidden), const),
                  pl.BlockSpec((1, hidden), const),
                  pl.BlockSpec((hidden, d), const),
                  pl.BlockSpec((1, d), const),
                  pl.BlockSpec((1, d), const)],
        out_specs=pl.BlockSpec((None, tm, d), tok),
        out_shape=jax.ShapeDtypeStruct((bsz, s, d), F32),
        scratch_shapes=[pltpu.VMEM((tm + SUBLANES, hidden), F32)],
        compiler_params=_params("arbitrary", "arbitrary"),
        name="ffn",
    )(x1, h2, mod, wup, wconv, bconv, wdown, g2, b2)


def _layer(x, mod, w_in, b_in, w_mlstm_conv, b_mlstm_conv, w_mlstm_q, w_mlstm_k, mlstm_norm_gain,
           w_mlstm_down, s5_lam_re, s5_lam_im, s5_log_dt, s5_b_re, s5_b_im, s5_c_re, s5_c_im, s5_d,
           w_s5_glu, w_mix_out, ln1_gain, ln1_bias, w_ffn_up, w_ffn_conv, b_ffn_conv, w_ffn_down,
           ln2_gain, ln2_bias, *, alpha):
    bsz, s, d = x.shape
    H = MLSTM_HEADS
    s5w = s5_d.shape[0]
    G = s5w // S5_GROUP
    tm = min(512, s)
    chunk = min(256, s)

    o_om, o_ip, o_fp, o_us = d, 2 * d, 2 * d + H, 2 * d + 2 * H
    o_ga, o_gb = o_us + s5w, o_us + s5w + d
    tok_cols = (slice(0, 2 * d), slice(o_ga, o_ga + d))
    chan_cols = (slice(o_us, o_us + s5w), slice(o_gb, o_gb + d), slice(o_ip, o_ip + 2 * H))
    wtok = jnp.concatenate([w_in[:, c].astype(BF16) for c in tok_cols], axis=1)
    btok = jnp.concatenate([b_in[c] for c in tok_cols]).reshape(1, -1)
    wt = jnp.concatenate([w_in[:, c].T.astype(BF16) for c in chan_cols], axis=0)
    bt = jnp.concatenate([b_in[c] for c in chan_cols]).reshape(-1, 1)

    xm, som, sga, ust, sgbt, gt = _inproj(x, mod, wtok, btok, wt, bt, tm=min(1024, s), s5w=s5w)

    hh = _mlstm(xm, gt, w_mlstm_conv, b_mlstm_conv.reshape(1, d), w_mlstm_q.astype(BF16),
                w_mlstm_k.astype(BF16), chunk=chunk)

    dup = lambda a: jnp.concatenate([a, a], axis=-1)
    P2 = 2 * S5_STATE
    NC = S5_GROUP
    pa = jnp.concatenate([dup(s5_lam_re)[:, None, :], dup(s5_lam_im)[:, None, :],
                          jnp.broadcast_to(s5_log_dt[:, None, None], (G, 1, P2)),
                          jnp.zeros((G, SUBLANES - 3, P2), F32)], axis=1)
    pb = jnp.concatenate([jnp.concatenate([jnp.swapaxes(s5_b_re, 1, 2), jnp.swapaxes(s5_b_im, 1, 2)], axis=-1),
                          dup(s5_c_re), dup(s5_c_im)], axis=1)
    crt = jnp.swapaxes(s5_c_re, 1, 2)
    cit = jnp.swapaxes(s5_c_im, 1, 2)
    dcol = jnp.concatenate([s5_d.reshape(G, NC, 1), jnp.zeros((G, P2 - NC, 1), F32)], axis=1)
    pc = jnp.concatenate([jnp.concatenate([crt, -cit], axis=1), jnp.concatenate([-cit, -crt], axis=1),
                          dup(s5_lam_re)[..., None], dup(s5_lam_im)[..., None], dcol,
                          jnp.zeros((G, P2, LANES - 2 * NC - 3), F32)], axis=2)
    yst = _s5(ust.reshape(bsz, s5w, s // LANES, LANES), pa, pb, pc).reshape(bsz, s5w, s)

    x1, h2 = _tail(x, mod, hh, som, mlstm_norm_gain.reshape(1, d), sga, yst, sgbt,
                   w_mlstm_down.astype(BF16), w_s5_glu.T.astype(BF16),
                   w_mix_out.astype(BF16), ln1_gain.reshape(1, d), ln1_bias.reshape(1, d), tm=tm, alpha=alpha)

    hidden = w_ffn_down.shape[0]
    return _ffn(x1, h2, mod, w_ffn_up.astype(BF16), w_ffn_conv, b_ffn_conv.reshape(1, hidden),
                w_ffn_down.astype(BF16), ln2_gain.reshape(1, d), ln2_bias.reshape(1, d),
                tm=tm, alpha=alpha)


def kernel(x, c, w_ada, b_ada, w_in, b_in, w_mlstm_conv, b_mlstm_conv, w_mlstm_q, w_mlstm_k, mlstm_norm_gain, w_mlstm_down, s5_lam_re, s5_lam_im, s5_log_dt, s5_b_re, s5_b_im, s5_c_re, s5_c_im, s5_d, w_s5_glu, w_mix_out, ln1_gain, ln1_bias, w_ffn_up, w_ffn_conv, b_ffn_conv, w_ffn_down, ln2_gain, ln2_bias):
    depth = w_ada.shape[0]
    alpha = (2.0 * depth) ** 0.25
    bsz, d = c.shape
    for l in range(depth):
        mod = _adaln(c, w_ada[l], b_ada[l]).reshape(bsz, 6, d)
        x = _layer(x, mod, w_in[l], b_in[l], w_mlstm_conv[l], b_mlstm_conv[l], w_mlstm_q[l], w_mlstm_k[l],
                   mlstm_norm_gain[l], w_mlstm_down[l], s5_lam_re[l], s5_lam_im[l], s5_log_dt[l], s5_b_re[l],
                   s5_b_im[l], s5_c_re[l], s5_c_im[l], s5_d[l], w_s5_glu[l], w_mix_out[l], ln1_gain[l],
                   ln1_bias[l], w_ffn_up[l], w_ffn_conv[l], b_ffn_conv[l], w_ffn_down[l], ln2_gain[l],
                   ln2_bias[l], alpha=alpha)
    return x
```

```python
import functools
import math

import jax
import jax.numpy as jnp
from jax import lax
from jax.experimental import pallas as pl
from jax.experimental.pallas import tpu as pltpu

F32 = jnp.float32
BF16 = jnp.bfloat16
HIGHEST = lax.Precision.HIGHEST

LN_EPS = 1e-5
MLSTM_HEADS = 4
MLSTM_CONV = 4
FFN_CONV = 3
S5_GROUP = 16
S5_STATE = 64
S5_CBLOCK = 2
TAIL_SUBTILES = 2
FFN_SUBTILES = 2
LANES = 128
SUBLANES = 8
VMEM_LIMIT_BYTES = 56 * 1024 * 1024

NT_DIMS = (((1,), (1,)), ((), ()))
TN_DIMS = (((0,), (0,)), ((), ()))


def _standardize(x):
    mu = jnp.mean(x, axis=-1, keepdims=True)
    xc = x - mu
    var = jnp.mean(xc * xc, axis=-1, keepdims=True)
    return xc * lax.rsqrt(var + LN_EPS)


def _sigmoid(x):
    return 1.0 / (1.0 + jnp.exp(-x))


def _gelu_tanh(x):
    return 0.5 * x * (1.0 + jnp.tanh(math.sqrt(2.0 / math.pi) * (x + 0.044715 * (x * x * x))))


def _params(*semantics):
    return pltpu.CompilerParams(dimension_semantics=semantics, vmem_limit_bytes=VMEM_LIMIT_BYTES)


def _adaln_kernel(c_ref, w_ref, b_ref, o_ref):
    c = c_ref[...]
    ca = c * _sigmoid(c)
    o_ref[...] = jnp.dot(ca, w_ref[...], precision=HIGHEST, preferred_element_type=F32) + b_ref[...]


def _adaln(c, w, b):
    bsz, d = c.shape
    n = w.shape[1]
    return pl.pallas_call(
        _adaln_kernel,
        grid=(n // d,),
        in_specs=[pl.BlockSpec((bsz, d), lambda j: (0, 0)),
                  pl.BlockSpec((d, d), lambda j: (0, j)),
                  pl.BlockSpec((1, d), lambda j: (0, j))],
        out_specs=pl.BlockSpec((bsz, d), lambda j: (0, j)),
        out_shape=jax.ShapeDtypeStruct((bsz, n), F32),
        compiler_params=_params("arbitrary"),
        name="adaln",
    )(c, w, b.reshape(1, n))


def _log_sigmoid(g):
    return jnp.minimum(g, 0.0) - jnp.log(1.0 + jnp.exp(-jnp.abs(g)))


def _inproj_kernel(x_ref, mod_ref, wtok_ref, btok_ref, wt_ref, bt_ref,
                   xm_ref, som_ref, sga_ref, ust_ref, sgbt_ref, gt_ref, *, d, s5w):
    x = x_ref[...]
    h = (_standardize(x) * (1.0 + mod_ref[1:2, :]) + mod_ref[0:1, :]).astype(BF16)
    p_xm = jnp.dot(h, wtok_ref[:, 0:d], preferred_element_type=F32)
    p_om = jnp.dot(h, wtok_ref[:, d:2 * d], preferred_element_type=F32)
    xm_ref[...] = (p_xm + btok_ref[:, 0:d]).astype(BF16)
    p_ga = jnp.dot(h, wtok_ref[:, 2 * d:3 * d], preferred_element_type=F32)
    som_ref[...] = _sigmoid(p_om + btok_ref[:, d:2 * d]).astype(BF16)
    pt_us = lax.dot_general(wt_ref[0:s5w, :], h, NT_DIMS, preferred_element_type=F32)
    sga_ref[...] = _sigmoid(p_ga + btok_ref[:, 2 * d:3 * d]).astype(BF16)
    pt_gb = lax.dot_general(wt_ref[s5w:s5w + d, :], h, NT_DIMS, preferred_element_type=F32)
    ust_ref[...] = (pt_us + bt_ref[0:s5w, :]).astype(BF16)
    ng = 2 * MLSTM_HEADS
    pt_g = lax.dot_general(wt_ref[s5w + d:s5w + d + ng, :], h, NT_DIMS, preferred_element_type=F32)
    sgbt_ref[...] = _sigmoid(pt_gb + bt_ref[s5w:s5w + d, :]).astype(BF16)
    gt_ref[...] = pt_g + bt_ref[s5w + d:s5w + d + ng, :]


def _inproj(x, mod, wtok, btok, wt, bt, *, tm, s5w):
    bsz, s, d = x.shape
    nt = wt.shape[0]
    ng = 2 * MLSTM_HEADS
    tok = lambda b, i: (b, i, 0)
    chan = lambda b, i: (b, 0, i)
    const = lambda b, i: (0, 0)
    return pl.pallas_call(
        functools.partial(_inproj_kernel, d=d, s5w=s5w),
        grid=(bsz, s // tm),
        in_specs=[pl.BlockSpec((None, tm, d), tok),
                  pl.BlockSpec((None, 6, d), lambda b, i: (b, 0, 0)),
                  pl.BlockSpec((d, 3 * d), const),
                  pl.BlockSpec((1, 3 * d), const),
                  pl.BlockSpec((nt, d), const),
                  pl.BlockSpec((nt, 1), const)],
        out_specs=[pl.BlockSpec((None, tm, d), tok),
                   pl.BlockSpec((None, tm, d), tok),
                   pl.BlockSpec((None, tm, d), tok),
                   pl.BlockSpec((None, s5w, tm), chan),
                   pl.BlockSpec((None, d, tm), chan),
                   pl.BlockSpec((None, ng, tm), chan)],
        out_shape=[jax.ShapeDtypeStruct((bsz, s, d), BF16),
                   jax.ShapeDtypeStruct((bsz, s, d), BF16),
                   jax.ShapeDtypeStruct((bsz, s, d), BF16),
                   jax.ShapeDtypeStruct((bsz, s5w, s), BF16),
                   jax.ShapeDtypeStruct((bsz, d, s), BF16),
                   jax.ShapeDtypeStruct((bsz, ng, s), F32)],
        compiler_params=_params("arbitrary", "arbitrary"),
        name="inproj",
    )(x, mod, wtok, btok, wt, bt)


def _lane_cumsum(x):
    n = x.shape[-1]
    lane = lax.broadcasted_iota(jnp.int32, x.shape, x.ndim - 1)
    sh = 1
    while sh < n:
        x = x + jnp.where(lane >= sh, pltpu.roll(x, sh, x.ndim - 1), 0.0)
        sh *= 2
    return x


def _mlstm_kernel(xm_ref, gt_ref, wconv_ref, bconv_ref, wq_ref, wk_ref,
                  out_ref, xbuf, c_ref, n_ref, m_ref, *, chunk, dv, dk):
    L = chunk
    H = MLSTM_HEADS

    @pl.when(pl.program_id(1) == 0)
    def _():
        xbuf[L:L + SUBLANES, :] = jnp.zeros((SUBLANES, xbuf.shape[1]), F32)
        c_ref[...] = jnp.zeros(c_ref.shape, F32)
        n_ref[...] = jnp.zeros(n_ref.shape, F32)
        m_ref[...] = jnp.zeros(m_ref.shape, F32)

    xbuf[0:SUBLANES, :] = xbuf[L:L + SUBLANES, :]
    xbuf[SUBLANES:SUBLANES + L, :] = xm_ref[...].astype(F32)

    ti = lax.broadcasted_iota(jnp.int32, (L, L), 0)
    si = lax.broadcasted_iota(jnp.int32, (L, L), 1)
    causal = si <= ti
    scale = dk ** -0.5

    c_old = [c_ref[h] for h in range(H)]
    n_old = [n_ref[h, 0:1, :] for h in range(H)]
    m_old = [m_ref[h, 0:1, 0:1] for h in range(H)]
    c_new, n_new, m_new_all = [], [], []

    heads = range(H)
    cols = [slice(h * dv, (h + 1) * dv) for h in heads]

    g = gt_ref[...]
    row = lax.broadcasted_iota(jnp.int32, g.shape, 0)
    bcum = _lane_cumsum(jnp.where(row >= H, _log_sigmoid(g), 0.0))
    r_all = jnp.where(row >= H, bcum, g)
    eye = (ti == si).astype(F32)
    c_all = lax.dot_general(eye, r_all, NT_DIMS, precision=HIGHEST, preferred_element_type=F32)
    ig_s = [r_all[h:h + 1, :] for h in heads]
    b_s = [r_all[H + h:H + h + 1, :] for h in heads]
    ig_t = [c_all[:, h:h + 1] for h in heads]
    b_t = [c_all[:, H + h:H + h + 1] for h in heads]
    b_last = [b_s[h][:, L - 1:L] for h in heads]

    xc, q, k, qb, kb, v = [], [], [], [], [], []
    for h in heads:
        acc = bconv_ref[:, cols[h]]
        for j in range(MLSTM_CONV):
            off = SUBLANES - (MLSTM_CONV - 1) + j
            acc = acc + wconv_ref[j:j + 1, cols[h]] * xbuf[off:off + L, cols[h]]
        xc.append((acc * _sigmoid(acc)).astype(BF16))
    for h in heads:
        q.append(jnp.dot(xc[h], wq_ref[h], preferred_element_type=F32) * scale)
        k.append(jnp.dot(xc[h], wk_ref[h], preferred_element_type=F32))
        qb.append(q[h].astype(BF16))
        kb.append(k[h].astype(BF16))
        v.append(xm_ref[:, cols[h]])

    m_t, sc_inter, wmat, s = [], [], [], []
    for h in heads:
        dmat = jnp.where(causal, b_t[h] - b_s[h] + ig_s[h], -jnp.inf)
        inter = b_t[h] + m_old[h]
        m_t.append(jnp.maximum(inter, jnp.max(dmat, axis=1, keepdims=True)))
        wmat.append(jnp.exp(dmat - m_t[h]))
        sc_inter.append(jnp.exp(inter - m_t[h]))
    for h in heads:
        s.append(lax.dot_general(qb[h], kb[h], NT_DIMS, preferred_element_type=F32) * wmat[h])

    for h in heads:
        g_s = b_last[h] - b_s[h] + ig_s[h]
        m_new = jnp.maximum(b_last[h] + m_old[h], jnp.max(g_s, axis=1, keepdims=True))
        wk_t = jnp.exp(b_last[h] - b_t[h] + ig_t[h] - m_new)
        decay = jnp.exp(b_last[h] + m_old[h] - m_new)
        kw = k[h] * wk_t
        c_new.append(decay * c_old[h]
                     + lax.dot_general(kw.astype(BF16), v[h], TN_DIMS, preferred_element_type=F32))
        n_new.append(decay * n_old[h] + jnp.sum(kw, axis=0, keepdims=True))
        m_new_all.append(m_new)

    for h in heads:
        num = (jnp.dot(s[h].astype(BF16), v[h], preferred_element_type=F32)
               + sc_inter[h] * jnp.dot(qb[h], c_old[h].astype(BF16), preferred_element_type=F32))
        den = (jnp.sum(s[h], axis=1, keepdims=True)
               + sc_inter[h] * jnp.sum(q[h] * n_old[h], axis=1, keepdims=True))
        out_ref[:, cols[h]] = (num / jnp.maximum(jnp.abs(den), jnp.exp(-m_t[h]))).astype(BF16)

    for h in range(H):
        c_ref[h] = c_new[h]
        n_ref[h, 0:1, :] = n_new[h]
        m_ref[h] = jnp.broadcast_to(m_new_all[h], m_ref.shape[1:])


def _mlstm(xm, gt, wconv, bconv, wq, wk, *, chunk):
    bsz, s, d = xm.shape
    H = MLSTM_HEADS
    dv = d // H
    dk = wq.shape[-1]
    ng = gt.shape[1]
    tok = lambda b, j: (b, j, 0)
    const2 = lambda b, j: (0, 0)
    const3 = lambda b, j: (0, 0, 0)
    return pl.pallas_call(
        functools.partial(_mlstm_kernel, chunk=chunk, dv=dv, dk=dk),
        grid=(bsz, s // chunk),
        in_specs=[pl.BlockSpec((None, chunk, d), tok),
                  pl.BlockSpec((None, ng, chunk), lambda b, j: (b, 0, j)),
                  pl.BlockSpec((MLSTM_CONV, d), const2),
                  pl.BlockSpec((1, d), const2),
                  pl.BlockSpec((H, dv, dk), const3),
                  pl.BlockSpec((H, dv, dk), const3)],
        out_specs=pl.BlockSpec((None, chunk, d), tok),
        out_shape=jax.ShapeDtypeStruct((bsz, s, d), BF16),
        scratch_shapes=[pltpu.VMEM((chunk + SUBLANES, d), F32),
                        pltpu.VMEM((H, dk, dv), F32),
                        pltpu.VMEM((H, SUBLANES, dk), F32),
                        pltpu.VMEM((H, SUBLANES, LANES), F32)],
        compiler_params=_params("arbitrary", "arbitrary"),
        name="mlstm",
    )(xm, gt, wconv, bconv, wq, wk)


def _s5_kernel(us_ref, ldt_ref, lamr_ref, lami_ref, lamc_ref, bt_ref, cc_ref, ca_ref, cb_ref, d_ref,
               out_ref, toep_ref, wb_ref, wc_ref, *, nb, nchunk):
    P = S5_STATE
    NC = S5_GROUP
    LS = LANES
    M = nb * nchunk

    dt = jnp.exp(ldt_ref[...])
    lam_re = lamr_ref[...]
    lam_im = lami_ref[...]
    lane2 = lax.broadcasted_iota(jnp.int32, (1, 2 * P), 1)
    sgn = jnp.where(lane2 < P, -1.0, 1.0)

    mag = jnp.exp(lam_re * dt)
    ar = mag * jnp.cos(lam_im * dt)
    ai = mag * jnp.sin(lam_im * dt)
    den = lam_re * lam_re + lam_im * lam_im
    zr = ((ar - 1.0) * lam_re + ai * lam_im) / den
    zi = (ai * lam_re - (ar - 1.0) * lam_im) / den
    b1 = bt_ref[...]
    bb = zr * b1 + zi * sgn * pltpu.roll(b1, P, 1)
    bb_sw = pltpu.roll(bb, P, 1)

    cr2 = cc_ref[0]
    ci2 = cc_ref[1]
    m12 = (cr2[:, None, :] * (bb * (-sgn))[None, :, :]
           - ci2[:, None, :] * bb_sw[None, :, :]).reshape(NC * NC, 2 * P)
    lam_re_c = lamc_ref[0]
    lam_im_c = lamc_ref[1]
    tau = lax.broadcasted_iota(jnp.int32, (2 * P, LS), 1).astype(F32)
    prow = lax.broadcasted_iota(jnp.int32, (2 * P, LS), 0)
    e0 = jnp.exp(lam_re_c * dt * tau)
    ang0 = lam_im_c * dt * tau
    pr0 = e0 * jnp.cos(ang0)
    pi0 = e0 * jnp.sin(ang0)
    pstack = jnp.where(prow < P, pr0, pi0)
    kpairs = jnp.dot(m12, pstack, precision=HIGHEST, preferred_element_type=F32)

    srow = lax.broadcasted_iota(jnp.int32, (LS, LS), 0)
    tcol = lax.broadcasted_iota(jnp.int32, (LS, LS), 1)
    lower = tcol >= srow
    srev = (LS - 1) - lax.broadcasted_iota(jnp.int32, (LS, 2 * P), 0).astype(F32)
    e1 = jnp.exp(lam_re * dt * srev)
    ang1 = lam_im * dt * srev
    r2 = e1 * jnp.cos(ang1)
    i2 = e1 * jnp.sin(ang1)
    bb_rot = bb_sw * sgn

    us = us_ref[...].reshape(nb, NC, nchunk, LS)
    y = None
    x_end = None
    for cp0 in range(0, NC, S5_CBLOCK):
        for cp in range(cp0, cp0 + S5_CBLOCK):
            for c in range(NC):
                r = c * NC + cp
                kv = jnp.broadcast_to(kpairs[r:r + 1, :], (LS, LS))
                tz = pltpu.roll(kv, 0, 1, stride=1, stride_axis=0)
                toep_ref[cp * LS:(cp + 1) * LS, c * LS:(c + 1) * LS] = jnp.where(lower, tz, 0.0).astype(BF16)
            wb_ref[cp * LS:(cp + 1) * LS, :] = (r2 * bb[cp:cp + 1, :] + i2 * bb_rot[cp:cp + 1, :]).astype(BF16)
        rows = slice(cp0 * LS, (cp0 + S5_CBLOCK) * LS)
        ub = jnp.concatenate(
            [jnp.concatenate([us[b, cp] for b in range(nb)], axis=0) for cp in range(cp0, cp0 + S5_CBLOCK)],
            axis=1)
        yp = jnp.dot(ub, toep_ref[rows, :], preferred_element_type=F32)
        xp = jnp.dot(ub, wb_ref[rows, :], preferred_element_type=F32)
        y = yp if y is None else y + yp
        x_end = xp if x_end is None else x_end + xp

    ar_c = jnp.exp(lam_re_c * dt) * jnp.cos(lam_im_c * dt)
    ai_c = jnp.exp(lam_re_c * dt) * jnp.sin(lam_im_c * dt)
    pr1 = pr0 * ar_c - pi0 * ai_c
    pi1 = pr0 * ai_c + pi0 * ar_c
    ca = ca_ref[...]
    cb = cb_ref[...]
    for c in range(NC):
        wc_ref[:, c * LS:(c + 1) * LS] = (ca[:, c:c + 1] * pr1 + cb[:, c:c + 1] * pi1).astype(BF16)

    jrow = lax.broadcasted_iota(jnp.int32, (M, 2 * P), 0) % nchunk
    lvl = lax.broadcasted_iota(jnp.int32, (SUBLANES, 2 * P), 0)
    nstep = (LS * jnp.left_shift(1, lvl)).astype(F32)
    el = jnp.exp(lam_re * dt * nstep)
    angl = lam_im * dt * nstep
    pl_all = el * jnp.cos(angl)
    ql_all = el * jnp.sin(angl) * sgn
    xs = x_end
    dstep = 1
    level = 0
    while dstep < nchunk:
        sh = jnp.where(jrow >= dstep, pltpu.roll(xs, dstep, 0), 0.0)
        xs = xs + pl_all[level:level + 1, :] * sh + ql_all[level:level + 1, :] * pltpu.roll(sh, P, 1)
        dstep *= 2
        level += 1
    x_prev = jnp.where(jrow >= 1, pltpu.roll(xs, 1, 0), 0.0)
    y = y + jnp.dot(x_prev.astype(BF16), wc_ref[...], preferred_element_type=F32)

    for c in range(NC):
        yc = y[:, c * LS:(c + 1) * LS]
        for b in range(nb):
            rows = slice(b * nchunk, (b + 1) * nchunk)
            yy = yc[rows, :] + d_ref[c:c + 1, 0:1] * us[b, c].astype(F32)
            out_ref[b, c] = _gelu_tanh(yy).astype(BF16)


def _s5(ust, ldt, lamr2, lami2, lamc, bt2, cc2, ca, cb, dcol):
    nb, w, s = ust.shape
    ls = LANES
    nchunk = s // ls
    G = w // S5_GROUP
    P2 = 2 * S5_STATE
    NC = S5_GROUP
    M = nb * nchunk
    g3 = lambda g: (g, 0, 0)
    g4 = lambda g: (g, 0, 0, 0)
    return pl.pallas_call(
        functools.partial(_s5_kernel, nb=nb, nchunk=nchunk),
        grid=(G,),
        in_specs=[pl.BlockSpec((nb, NC, nchunk * ls), lambda g: (0, g, 0)),
                  pl.BlockSpec((None, 1, 1), g3),
                  pl.BlockSpec((None, 1, P2), g3),
                  pl.BlockSpec((None, 1, P2), g3),
                  pl.BlockSpec((None, 2, P2, 1), g4),
                  pl.BlockSpec((None, NC, P2), g3),
                  pl.BlockSpec((None, 2, NC, P2), g4),
                  pl.BlockSpec((None, P2, NC), g3),
                  pl.BlockSpec((None, P2, NC), g3),
                  pl.BlockSpec((None, NC, 1), g3)],
        out_specs=pl.BlockSpec((nb, NC, nchunk, ls), lambda g: (0, g, 0, 0)),
        out_shape=jax.ShapeDtypeStruct((nb, w, nchunk, ls), BF16),
        scratch_shapes=[pltpu.VMEM((NC * ls, NC * ls), BF16),
                        pltpu.VMEM((NC * ls, P2), BF16),
                        pltpu.VMEM((P2, NC * ls), BF16)],
        compiler_params=_params("arbitrary"),
        name="s5",
    )(ust, ldt, lamr2, lami2, lamc, bt2, cc2, ca, cb, dcol)


def _tail_kernel(x_ref, mod_ref, hh_ref, som_ref, gain_ref, sga_ref, yst_ref, sgbt_ref, wdown_ref, wglut_ref,
                 wout_ref, g1_ref, b1_ref, x1_ref, h2_ref, *, d, alpha):
    dv = d // MLSTM_HEADS
    tm = x_ref.shape[0]
    sub = tm // TAIL_SUBTILES
    rows = [slice(i * sub, (i + 1) * sub) for i in range(TAIL_SUBTILES)]
    vgt = [jnp.dot(wglut_ref[...], yst_ref[:, r], preferred_element_type=F32) for r in rows]
    ya = [jnp.concatenate(
        [(_standardize(hh_ref[r, h * dv:(h + 1) * dv].astype(F32)) * gain_ref[:, h * dv:(h + 1) * dv]
          * som_ref[r, h * dv:(h + 1) * dv].astype(F32)).astype(BF16) for h in range(MLSTM_HEADS)], axis=1)
        for r in rows]
    y_a = [jnp.dot(ya[i], wdown_ref[...], preferred_element_type=F32) for i in range(TAIL_SUBTILES)]
    z = []
    for i, r in enumerate(rows):
        zbt = vgt[i][0:d, :] * _sigmoid(vgt[i][d:2 * d, :]) * sgbt_ref[:, r].astype(F32)
        z.append((sga_ref[r, :].astype(F32) * y_a[i] + zbt.T).astype(BF16))
    mix = [jnp.dot(z[i], wout_ref[...], preferred_element_type=F32) for i in range(TAIL_SUBTILES)]
    for i, r in enumerate(rows):
        res = alpha * x_ref[r, :] + (1.0 + mod_ref[2:3, :]) * mix[i]
        x1 = _standardize(res) * g1_ref[...] + b1_ref[...]
        x1_ref[r, :] = x1
        h2_ref[r, :] = (_standardize(x1) * (1.0 + mod_ref[4:5, :]) + mod_ref[3:4, :]).astype(BF16)


def _tail(x, mod, hh, som, gain, sga, yst, sgbt, wdown, wglut, wout, g1, b1, *, tm, alpha):
    bsz, s, d = x.shape
    s5w = yst.shape[1]
    tok = lambda b, i: (b, i, 0)
    chan = lambda b, i: (b, 0, i)
    const = lambda b, i: (0, 0)
    return pl.pallas_call(
        functools.partial(_tail_kernel, d=d, alpha=alpha),
        grid=(bsz, s // tm),
        in_specs=[pl.BlockSpec((None, tm, d), tok),
                  pl.BlockSpec((None, 6, d), lambda b, i: (b, 0, 0)),
                  pl.BlockSpec((None, tm, d), tok),
                  pl.BlockSpec((None, tm, d), tok),
                  pl.BlockSpec((1, d), const),
                  pl.BlockSpec((None, tm, d), tok),
                  pl.BlockSpec((None, s5w, tm), chan),
                  pl.BlockSpec((None, d, tm), chan),
                  pl.BlockSpec((d, d), const),
                  pl.BlockSpec((2 * d, s5w), const),
                  pl.BlockSpec((d, d), const),
                  pl.BlockSpec((1, d), const),
                  pl.BlockSpec((1, d), const)],
        out_specs=[pl.BlockSpec((None, tm, d), tok),
                   pl.BlockSpec((None, tm, d), tok)],
        out_shape=[jax.ShapeDtypeStruct((bsz, s, d), F32),
                   jax.ShapeDtypeStruct((bsz, s, d), BF16)],
        compiler_params=_params("arbitrary", "arbitrary"),
        name="tail",
    )(x, mod, hh, som, gain, sga, yst, sgbt, wdown, wglut, wout, g1, b1)


def _ffn_kernel(x1_ref, h2_ref, mod_ref, wup_ref, wconv_ref, bconv_ref, wdown_ref, g2_ref, b2_ref,
                out_ref, gbuf, *, hidden, alpha):
    tm = h2_ref.shape[0]

    @pl.when(pl.program_id(1) == 0)
    def _():
        gbuf[tm:tm + SUBLANES, :] = jnp.zeros((SUBLANES, hidden), F32)

    sub = tm // FFN_SUBTILES
    rows = [slice(i * sub, (i + 1) * sub) for i in range(FFN_SUBTILES)]
    gbuf[0:SUBLANES, :] = gbuf[tm:tm + SUBLANES, :]
    gate = []
    for i, r in enumerate(rows):
        gate.append(jnp.dot(h2_ref[r, :], wup_ref[:, hidden:2 * hidden], preferred_element_type=F32))
        gbuf[SUBLANES + i * sub:SUBLANES + (i + 1) * sub, :] = gate[i]
    act = []
    for i, r in enumerate(rows):
        conv = bconv_ref[...] + wconv_ref[FFN_CONV - 1:FFN_CONV, :] * gate[i]
        for k in range(1, FFN_CONV):
            off = SUBLANES + i * sub - k
            conv = conv + wconv_ref[FFN_CONV - 1 - k:FFN_CONV - k, :] * gbuf[off:off + sub, :]
        gact = _gelu_tanh(conv)
        val = jnp.dot(h2_ref[r, :], wup_ref[:, 0:hidden], preferred_element_type=F32)
        act.append((gact * val).astype(BF16))
    down = [jnp.dot(act[i], wdown_ref[...], preferred_element_type=F32) for i in range(FFN_SUBTILES)]
    for i, r in enumerate(rows):
        res = alpha * x1_ref[r, :] + (1.0 + mod_ref[5:6, :]) * down[i]
        out_ref[r, :] = _standardize(res) * g2_ref[...] + b2_ref[...]


def _ffn(x1, h2, mod, wup, wconv, bconv, wdown, g2, b2, *, tm, alpha):
    bsz, s, d = x1.shape
    hidden = wdown.shape[0]
    tok = lambda b, i: (b, i, 0)
    const = lambda b, i: (0, 0)
    return pl.pallas_call(
        functools.partial(_ffn_kernel, hidden=hidden, alpha=alpha),
        grid=(bsz, s // tm),
        in_specs=[pl.BlockSpec((None, tm, d), tok),
                  pl.BlockSpec((None, tm, d), tok),
                  pl.BlockSpec((None, 6, d), lambda b, i: (b, 0, 0)),
                  pl.BlockSpec((d, 2 * hidden), const),
                  pl.BlockSpec((FFN_CONV, hidden), const),
                  pl.BlockSpec((1, hidden), const),
                  pl.BlockSpec((hidden, d), const),
                  pl.BlockSpec((1, d), const),
                  pl.BlockSpec((1, d), const)],
        out_specs=pl.BlockSpec((None, tm, d), tok),
        out_shape=jax.ShapeDtypeStruct((bsz, s, d), F32),
        scratch_shapes=[pltpu.VMEM((tm + SUBLANES, hidden), F32)],
        compiler_params=_params("arbitrary", "arbitrary"),
        name="ffn",
    )(x1, h2, mod, wup, wconv, bconv, wdown, g2, b2)


def _layer(x, mod, w_in, b_in, w_mlstm_conv, b_mlstm_conv, w_mlstm_q, w_mlstm_k, mlstm_norm_gain,
           w_mlstm_down, s5_lam_re, s5_lam_im, s5_log_dt, s5_b_re, s5_b_im, s5_c_re, s5_c_im, s5_d,
           w_s5_glu, w_mix_out, ln1_gain, ln1_bias, w_ffn_up, w_ffn_conv, b_ffn_conv, w_ffn_down,
           ln2_gain, ln2_bias, *, alpha):
    bsz, s, d = x.shape
    H = MLSTM_HEADS
    s5w = s5_d.shape[0]
    G = s5w // S5_GROUP
    tm = min(512, s)
    chunk = min(256, s)

    o_om, o_ip, o_fp, o_us = d, 2 * d, 2 * d + H, 2 * d + 2 * H
    o_ga, o_gb = o_us + s5w, o_us + s5w + d
    tok_cols = (slice(0, 2 * d), slice(o_ga, o_ga + d))
    chan_cols = (slice(o_us, o_us + s5w), slice(o_gb, o_gb + d), slice(o_ip, o_ip + 2 * H))
    wtok = jnp.concatenate([w_in[:, c].astype(BF16) for c in tok_cols], axis=1)
    btok = jnp.concatenate([b_in[c] for c in tok_cols]).reshape(1, -1)
    wt = jnp.concatenate([w_in[:, c].T.astype(BF16) for c in chan_cols], axis=0)
    bt = jnp.concatenate([b_in[c] for c in chan_cols]).reshape(-1, 1)

    xm, som, sga, ust, sgbt, gt = _inproj(x, mod, wtok, btok, wt, bt, tm=min(1024, s), s5w=s5w)

    hh = _mlstm(xm, gt, w_mlstm_conv, b_mlstm_conv.reshape(1, d), w_mlstm_q.astype(BF16),
                w_mlstm_k.astype(BF16), chunk=chunk)

    dup = lambda a: jnp.concatenate([a, a], axis=-1)
    lamr2 = dup(s5_lam_re)[:, None, :]
    lami2 = dup(s5_lam_im)[:, None, :]
    lamc = jnp.stack([dup(s5_lam_re), dup(s5_lam_im)], axis=1)[..., None]
    bt2 = jnp.concatenate([jnp.swapaxes(s5_b_re, 1, 2), jnp.swapaxes(s5_b_im, 1, 2)], axis=-1)
    cc2 = jnp.stack([dup(s5_c_re), dup(s5_c_im)], axis=1)
    crt = jnp.swapaxes(s5_c_re, 1, 2)
    cit = jnp.swapaxes(s5_c_im, 1, 2)
    ca = jnp.concatenate([crt, -cit], axis=1)
    cb = jnp.concatenate([-cit, -crt], axis=1)
    yst = _s5(ust, s5_log_dt.reshape(G, 1, 1), lamr2, lami2, lamc,
              bt2, cc2, ca, cb, s5_d.reshape(G, S5_GROUP, 1)).reshape(bsz, s5w, s)

    x1, h2 = _tail(x, mod, hh, som, mlstm_norm_gain.reshape(1, d), sga, yst, sgbt,
                   w_mlstm_down.astype(BF16), w_s5_glu.T.astype(BF16),
                   w_mix_out.astype(BF16), ln1_gain.reshape(1, d), ln1_bias.reshape(1, d), tm=tm, alpha=alpha)

    hidden = w_ffn_down.shape[0]
    return _ffn(x1, h2, mod, w_ffn_up.astype(BF16), w_ffn_conv, b_ffn_conv.reshape(1, hidden),
                w_ffn_down.astype(BF16), ln2_gain.reshape(1, d), ln2_bias.reshape(1, d),
                tm=tm, alpha=alpha)


def kernel(x, c, w_ada, b_ada, w_in, b_in, w_mlstm_conv, b_mlstm_conv, w_mlstm_q, w_mlstm_k, mlstm_norm_gain, w_mlstm_down, s5_lam_re, s5_lam_im, s5_log_dt, s5_b_re, s5_b_im, s5_c_re, s5_c_im, s5_d, w_s5_glu, w_mix_out, ln1_gain, ln1_bias, w_ffn_up, w_ffn_conv, b_ffn_conv, w_ffn_down, ln2_gain, ln2_bias):
    depth = w_ada.shape[0]
    alpha = (2.0 * depth) ** 0.25
    bsz, d = c.shape
    for l in range(depth):
        mod = _adaln(c, w_ada[l], b_ada[l]).reshape(bsz, 6, d)
        x = _layer(x, mod, w_in[l], b_in[l], w_mlstm_conv[l], b_mlstm_conv[l], w_mlstm_q[l], w_mlstm_k[l],
                   mlstm_norm_gain[l], w_mlstm_down[l], s5_lam_re[l], s5_lam_im[l], s5_log_dt[l], s5_b_re[l],
                   s5_b_im[l], s5_c_re[l], s5_c_im[l], s5_d[l], w_s5_glu[l], w_mix_out[l], ln1_gain[l],
                   ln1_bias[l], w_ffn_up[l], w_ffn_conv[l], b_ffn_conv[l], w_ffn_down[l], ln2_gain[l],
                   ln2_bias[l], alpha=alpha)
    return x
```

```python
import functools
import math

import jax
import jax.numpy as jnp
from jax import lax
from jax.experimental import pallas as pl
from jax.experimental.pallas import tpu as pltpu

F32 = jnp.float32
BF16 = jnp.bfloat16
HIGHEST = lax.Precision.HIGHEST

LN_EPS = 1e-5
MLSTM_HEADS = 4
MLSTM_CONV = 4
FFN_CONV = 3
S5_GROUP = 16
S5_STATE = 64
S5_CBLOCK = 2
TAIL_SUBTILES = 2
FFN_SUBTILES = 2
LANES = 128
SUBLANES = 8
VMEM_LIMIT_BYTES = 56 * 1024 * 1024

NT_DIMS = (((1,), (1,)), ((), ()))
TN_DIMS = (((0,), (0,)), ((), ()))


def _standardize(x):
    mu = jnp.mean(x, axis=-1, keepdims=True)
    xc = x - mu
    var = jnp.mean(xc * xc, axis=-1, keepdims=True)
    return xc * lax.rsqrt(var + LN_EPS)


def _sigmoid(x):
    return 1.0 / (1.0 + jnp.exp(-x))


def _gelu_tanh(x):
    return 0.5 * x * (1.0 + jnp.tanh(math.sqrt(2.0 / math.pi) * (x + 0.044715 * (x * x * x))))


def _params(*semantics):
    return pltpu.CompilerParams(dimension_semantics=semantics, vmem_limit_bytes=VMEM_LIMIT_BYTES)


def _adaln_kernel(c_ref, w_ref, b_ref, o_ref):
    c = c_ref[...]
    ca = c * _sigmoid(c)
    o_ref[...] = jnp.dot(ca, w_ref[...], precision=HIGHEST, preferred_element_type=F32) + b_ref[...]


def _adaln(c, w, b):
    bsz, d = c.shape
    n = w.shape[1]
    return pl.pallas_call(
        _adaln_kernel,
        grid=(n // d,),
        in_specs=[pl.BlockSpec((bsz, d), lambda j: (0, 0)),
                  pl.BlockSpec((d, d), lambda j: (0, j)),
                  pl.BlockSpec((1, d), lambda j: (0, j))],
        out_specs=pl.BlockSpec((bsz, d), lambda j: (0, j)),
        out_shape=jax.ShapeDtypeStruct((bsz, n), F32),
        compiler_params=_params("arbitrary"),
        name="adaln",
    )(c, w, b.reshape(1, n))


def _log_sigmoid(g):
    return jnp.minimum(g, 0.0) - jnp.log(1.0 + jnp.exp(-jnp.abs(g)))


def _inproj_kernel(x_ref, mod_ref, wtok_ref, btok_ref, wt_ref, bt_ref,
                   xm_ref, som_ref, sga_ref, ust_ref, sgbt_ref, gt_ref, *, d, s5w):
    x = x_ref[...]
    h = (_standardize(x) * (1.0 + mod_ref[1:2, :]) + mod_ref[0:1, :]).astype(BF16)
    p_xm = jnp.dot(h, wtok_ref[:, 0:d], preferred_element_type=F32)
    p_om = jnp.dot(h, wtok_ref[:, d:2 * d], preferred_element_type=F32)
    xm_ref[...] = (p_xm + btok_ref[:, 0:d]).astype(BF16)
    p_ga = jnp.dot(h, wtok_ref[:, 2 * d:3 * d], preferred_element_type=F32)
    som_ref[...] = _sigmoid(p_om + btok_ref[:, d:2 * d]).astype(BF16)
    pt_us = lax.dot_general(wt_ref[0:s5w, :], h, NT_DIMS, preferred_element_type=F32)
    sga_ref[...] = _sigmoid(p_ga + btok_ref[:, 2 * d:3 * d]).astype(BF16)
    pt_gb = lax.dot_general(wt_ref[s5w:s5w + d, :], h, NT_DIMS, preferred_element_type=F32)
    ust_ref[...] = (pt_us + bt_ref[0:s5w, :]).astype(BF16)
    ng = 2 * MLSTM_HEADS
    pt_g = lax.dot_general(wt_ref[s5w + d:s5w + d + ng, :], h, NT_DIMS, preferred_element_type=F32)
    sgbt_ref[...] = _sigmoid(pt_gb + bt_ref[s5w:s5w + d, :]).astype(BF16)
    gt_ref[...] = pt_g + bt_ref[s5w + d:s5w + d + ng, :]


def _inproj(x, mod, wtok, btok, wt, bt, *, tm, s5w):
    bsz, s, d = x.shape
    nt = wt.shape[0]
    ng = 2 * MLSTM_HEADS
    tok = lambda b, i: (b, i, 0)
    chan = lambda b, i: (b, 0, i)
    const = lambda b, i: (0, 0)
    return pl.pallas_call(
        functools.partial(_inproj_kernel, d=d, s5w=s5w),
        grid=(bsz, s // tm),
        in_specs=[pl.BlockSpec((None, tm, d), tok),
                  pl.BlockSpec((None, 6, d), lambda b, i: (b, 0, 0)),
                  pl.BlockSpec((d, 3 * d), const),
                  pl.BlockSpec((1, 3 * d), const),
                  pl.BlockSpec((nt, d), const),
                  pl.BlockSpec((nt, 1), const)],
        out_specs=[pl.BlockSpec((None, tm, d), tok),
                   pl.BlockSpec((None, tm, d), tok),
                   pl.BlockSpec((None, tm, d), tok),
                   pl.BlockSpec((None, s5w, tm), chan),
                   pl.BlockSpec((None, d, tm), chan),
                   pl.BlockSpec((None, ng, tm), chan)],
        out_shape=[jax.ShapeDtypeStruct((bsz, s, d), BF16),
                   jax.ShapeDtypeStruct((bsz, s, d), BF16),
                   jax.ShapeDtypeStruct((bsz, s, d), BF16),
                   jax.ShapeDtypeStruct((bsz, s5w, s), BF16),
                   jax.ShapeDtypeStruct((bsz, d, s), BF16),
                   jax.ShapeDtypeStruct((bsz, ng, s), F32)],
        compiler_params=_params("arbitrary", "arbitrary"),
        name="inproj",
    )(x, mod, wtok, btok, wt, bt)


def _lane_cumsum(x):
    n = x.shape[-1]
    lane = lax.broadcasted_iota(jnp.int32, x.shape, x.ndim - 1)
    sh = 1
    while sh < n:
        x = x + jnp.where(lane >= sh, pltpu.roll(x, sh, x.ndim - 1), 0.0)
        sh *= 2
    return x


def _mlstm_kernel(xm_ref, gt_ref, wconv_ref, bconv_ref, wq_ref, wk_ref,
                  out_ref, xbuf, c_ref, n_ref, m_ref, *, chunk, dv, dk):
    L = chunk
    H = MLSTM_HEADS

    @pl.when(pl.program_id(1) == 0)
    def _():
        xbuf[L:L + SUBLANES, :] = jnp.zeros((SUBLANES, xbuf.shape[1]), F32)
        c_ref[...] = jnp.zeros(c_ref.shape, F32)
        n_ref[...] = jnp.zeros(n_ref.shape, F32)
        m_ref[...] = jnp.zeros(m_ref.shape, F32)

    xbuf[0:SUBLANES, :] = xbuf[L:L + SUBLANES, :]
    xbuf[SUBLANES:SUBLANES + L, :] = xm_ref[...].astype(F32)

    ti = lax.broadcasted_iota(jnp.int32, (L, L), 0)
    si = lax.broadcasted_iota(jnp.int32, (L, L), 1)
    causal = si <= ti
    scale = dk ** -0.5

    c_old = [c_ref[h] for h in range(H)]
    n_old = [n_ref[h, 0:1, :] for h in range(H)]
    m_old = [m_ref[h, 0:1, 0:1] for h in range(H)]
    c_new, n_new, m_new_all = [], [], []

    heads = range(H)
    cols = [slice(h * dv, (h + 1) * dv) for h in heads]

    g = gt_ref[...]
    row = lax.broadcasted_iota(jnp.int32, g.shape, 0)
    bcum = _lane_cumsum(jnp.where(row >= H, _log_sigmoid(g), 0.0))
    r_all = jnp.where(row >= H, bcum, g)
    eye = (ti == si).astype(F32)
    c_all = lax.dot_general(eye, r_all, NT_DIMS, precision=HIGHEST, preferred_element_type=F32)
    ig_s = [r_all[h:h + 1, :] for h in heads]
    b_s = [r_all[H + h:H + h + 1, :] for h in heads]
    ig_t = [c_all[:, h:h + 1] for h in heads]
    b_t = [c_all[:, H + h:H + h + 1] for h in heads]
    b_last = [b_s[h][:, L - 1:L] for h in heads]

    xc, q, k, qb, kb, v = [], [], [], [], [], []
    for h in heads:
        acc = bconv_ref[:, cols[h]]
        for j in range(MLSTM_CONV):
            off = SUBLANES - (MLSTM_CONV - 1) + j
            acc = acc + wconv_ref[j:j + 1, cols[h]] * xbuf[off:off + L, cols[h]]
        xc.append((acc * _sigmoid(acc)).astype(BF16))
    for h in heads:
        q.append(jnp.dot(xc[h], wq_ref[h], preferred_element_type=F32) * scale)
        k.append(jnp.dot(xc[h], wk_ref[h], preferred_element_type=F32))
        qb.append(q[h].astype(BF16))
        kb.append(k[h].astype(BF16))
        v.append(xm_ref[:, cols[h]])

    m_t, sc_inter, wmat, s = [], [], [], []
    for h in heads:
        dmat = jnp.where(causal, b_t[h] - b_s[h] + ig_s[h], -jnp.inf)
        inter = b_t[h] + m_old[h]
        m_t.append(jnp.maximum(inter, jnp.max(dmat, axis=1, keepdims=True)))
        wmat.append(jnp.exp(dmat - m_t[h]))
        sc_inter.append(jnp.exp(inter - m_t[h]))
    for h in heads:
        s.append(lax.dot_general(qb[h], kb[h], NT_DIMS, preferred_element_type=F32) * wmat[h])

    for h in heads:
        g_s = b_last[h] - b_s[h] + ig_s[h]
        m_new = jnp.maximum(b_last[h] + m_old[h], jnp.max(g_s, axis=1, keepdims=True))
        wk_t = jnp.exp(b_last[h] - b_t[h] + ig_t[h] - m_new)
        decay = jnp.exp(b_last[h] + m_old[h] - m_new)
        kw = k[h] * wk_t
        c_new.append(decay * c_old[h]
                     + lax.dot_general(kw.astype(BF16), v[h], TN_DIMS, preferred_element_type=F32))
        n_new.append(decay * n_old[h] + jnp.sum(kw, axis=0, keepdims=True))
        m_new_all.append(m_new)

    for h in heads:
        num = (jnp.dot(s[h].astype(BF16), v[h], preferred_element_type=F32)
               + sc_inter[h] * jnp.dot(qb[h], c_old[h].astype(BF16), preferred_element_type=F32))
        den = (jnp.sum(s[h], axis=1, keepdims=True)
               + sc_inter[h] * jnp.sum(q[h] * n_old[h], axis=1, keepdims=True))
        out_ref[:, cols[h]] = (num / jnp.maximum(jnp.abs(den), jnp.exp(-m_t[h]))).astype(BF16)

    for h in range(H):
        c_ref[h] = c_new[h]
        n_ref[h, 0:1, :] = n_new[h]
        m_ref[h] = jnp.broadcast_to(m_new_all[h], m_ref.shape[1:])


def _mlstm(xm, gt, wconv, bconv, wq, wk, *, chunk):
    bsz, s, d = xm.shape
    H = MLSTM_HEADS
    dv = d // H
    dk = wq.shape[-1]
    ng = gt.shape[1]
    tok = lambda b, j: (b, j, 0)
    const2 = lambda b, j: (0, 0)
    const3 = lambda b, j: (0, 0, 0)
    return pl.pallas_call(
        functools.partial(_mlstm_kernel, chunk=chunk, dv=dv, dk=dk),
        grid=(bsz, s // chunk),
        in_specs=[pl.BlockSpec((None, chunk, d), tok),
                  pl.BlockSpec((None, ng, chunk), lambda b, j: (b, 0, j)),
                  pl.BlockSpec((MLSTM_CONV, d), const2),
                  pl.BlockSpec((1, d), const2),
                  pl.BlockSpec((H, dv, dk), const3),
                  pl.BlockSpec((H, dv, dk), const3)],
        out_specs=pl.BlockSpec((None, chunk, d), tok),
        out_shape=jax.ShapeDtypeStruct((bsz, s, d), BF16),
        scratch_shapes=[pltpu.VMEM((chunk + SUBLANES, d), F32),
                        pltpu.VMEM((H, dk, dv), F32),
                        pltpu.VMEM((H, SUBLANES, dk), F32),
                        pltpu.VMEM((H, SUBLANES, LANES), F32)],
        compiler_params=_params("arbitrary", "arbitrary"),
        name="mlstm",
    )(xm, gt, wconv, bconv, wq, wk)


def _s5_kernel(us_ref, ldt_ref, lamr_ref, lami_ref, lamc_ref, bt_ref, cc_ref, ca_ref, cb_ref, d_ref,
               out_ref, toep_ref, wb_ref, wc_ref, ybuf, *, nb, nchunk):
    P = S5_STATE
    NC = S5_GROUP
    LS = LANES
    M = nb * nchunk

    dt = jnp.exp(ldt_ref[...])
    lam_re = lamr_ref[...]
    lam_im = lami_ref[...]
    lane2 = lax.broadcasted_iota(jnp.int32, (1, 2 * P), 1)
    sgn = jnp.where(lane2 < P, -1.0, 1.0)

    mag = jnp.exp(lam_re * dt)
    ar = mag * jnp.cos(lam_im * dt)
    ai = mag * jnp.sin(lam_im * dt)
    den = lam_re * lam_re + lam_im * lam_im
    zr = ((ar - 1.0) * lam_re + ai * lam_im) / den
    zi = (ai * lam_re - (ar - 1.0) * lam_im) / den
    b1 = bt_ref[...]
    bb = zr * b1 + zi * sgn * pltpu.roll(b1, P, 1)
    bb_sw = pltpu.roll(bb, P, 1)

    cr2 = cc_ref[0]
    ci2 = cc_ref[1]
    m12 = (cr2[:, None, :] * (bb * (-sgn))[None, :, :]
           - ci2[:, None, :] * bb_sw[None, :, :]).reshape(NC * NC, 2 * P)
    lam_re_c = lamc_ref[0]
    lam_im_c = lamc_ref[1]
    tau = lax.broadcasted_iota(jnp.int32, (2 * P, LS), 1).astype(F32)
    prow = lax.broadcasted_iota(jnp.int32, (2 * P, LS), 0)
    e0 = jnp.exp(lam_re_c * dt * tau)
    ang0 = lam_im_c * dt * tau
    pr0 = e0 * jnp.cos(ang0)
    pi0 = e0 * jnp.sin(ang0)
    pstack = jnp.where(prow < P, pr0, pi0)
    kpairs = jnp.dot(m12, pstack, precision=HIGHEST, preferred_element_type=F32)

    srow = lax.broadcasted_iota(jnp.int32, (LS, LS), 0)
    tcol = lax.broadcasted_iota(jnp.int32, (LS, LS), 1)
    lower = tcol >= srow
    srev = (LS - 1) - lax.broadcasted_iota(jnp.int32, (LS, 2 * P), 0).astype(F32)
    e1 = jnp.exp(lam_re * dt * srev)
    ang1 = lam_im * dt * srev
    r2 = e1 * jnp.cos(ang1)
    i2 = e1 * jnp.sin(ang1)
    bb_rot = bb_sw * sgn

    us = us_ref[...].reshape(nb, NC, nchunk, LS)
    y = None
    x_end = None
    for cp0 in range(0, NC, S5_CBLOCK):
        for cp in range(cp0, cp0 + S5_CBLOCK):
            for c in range(NC):
                r = c * NC + cp
                kv = jnp.broadcast_to(kpairs[r:r + 1, :], (LS, LS))
                tz = pltpu.roll(kv, 0, 1, stride=1, stride_axis=0)
                toep_ref[cp * LS:(cp + 1) * LS, c * LS:(c + 1) * LS] = jnp.where(lower, tz, 0.0).astype(BF16)
            wb_ref[cp * LS:(cp + 1) * LS, :] = (r2 * bb[cp:cp + 1, :] + i2 * bb_rot[cp:cp + 1, :]).astype(BF16)
        rows = slice(cp0 * LS, (cp0 + S5_CBLOCK) * LS)
        ub = jnp.concatenate(
            [jnp.concatenate([us[b, cp] for b in range(nb)], axis=0) for cp in range(cp0, cp0 + S5_CBLOCK)],
            axis=1)
        yp = jnp.dot(ub, toep_ref[rows, :], preferred_element_type=F32)
        xp = jnp.dot(ub, wb_ref[rows, :], preferred_element_type=F32)
        y = yp if y is None else y + yp
        x_end = xp if x_end is None else x_end + xp

    ar_c = jnp.exp(lam_re_c * dt) * jnp.cos(lam_im_c * dt)
    ai_c = jnp.exp(lam_re_c * dt) * jnp.sin(lam_im_c * dt)
    pr1 = pr0 * ar_c - pi0 * ai_c
    pi1 = pr0 * ai_c + pi0 * ar_c
    ca = ca_ref[...]
    cb = cb_ref[...]
    for c in range(NC):
        wc_ref[:, c * LS:(c + 1) * LS] = (ca[:, c:c + 1] * pr1 + cb[:, c:c + 1] * pi1).astype(BF16)

    jrow = lax.broadcasted_iota(jnp.int32, (M, 2 * P), 0) % nchunk
    lvl = lax.broadcasted_iota(jnp.int32, (SUBLANES, 2 * P), 0)
    nstep = (LS * jnp.left_shift(1, lvl)).astype(F32)
    el = jnp.exp(lam_re * dt * nstep)
    angl = lam_im * dt * nstep
    pl_all = el * jnp.cos(angl)
    ql_all = el * jnp.sin(angl) * sgn
    xs = x_end
    dstep = 1
    level = 0
    while dstep < nchunk:
        sh = jnp.where(jrow >= dstep, pltpu.roll(xs, dstep, 0), 0.0)
        xs = xs + pl_all[level:level + 1, :] * sh + ql_all[level:level + 1, :] * pltpu.roll(sh, P, 1)
        dstep *= 2
        level += 1
    x_prev = jnp.where(jrow >= 1, pltpu.roll(xs, 1, 0), 0.0)
    y = y + jnp.dot(x_prev.astype(BF16), wc_ref[...], preferred_element_type=F32)

    for c in range(NC):
        yc = y[:, c * LS:(c + 1) * LS]
        for b in range(nb):
            rows = slice(b * nchunk, (b + 1) * nchunk)
            yy = yc[rows, :] + d_ref[c:c + 1, 0:1] * us[b, c].astype(F32)
            ybuf[b, c] = _gelu_tanh(yy).astype(BF16)
    out_ref[...] = ybuf[...].reshape(nb, NC, nchunk * LS)


def _s5(ust, ldt, lamr2, lami2, lamc, bt2, cc2, ca, cb, dcol):
    nb, w, s = ust.shape
    ls = LANES
    nchunk = s // ls
    G = w // S5_GROUP
    P2 = 2 * S5_STATE
    NC = S5_GROUP
    M = nb * nchunk
    g3 = lambda g: (g, 0, 0)
    g4 = lambda g: (g, 0, 0, 0)
    return pl.pallas_call(
        functools.partial(_s5_kernel, nb=nb, nchunk=nchunk),
        grid=(G,),
        in_specs=[pl.BlockSpec((nb, NC, nchunk * ls), lambda g: (0, g, 0)),
                  pl.BlockSpec((None, 1, 1), g3),
                  pl.BlockSpec((None, 1, P2), g3),
                  pl.BlockSpec((None, 1, P2), g3),
                  pl.BlockSpec((None, 2, P2, 1), g4),
                  pl.BlockSpec((None, NC, P2), g3),
                  pl.BlockSpec((None, 2, NC, P2), g4),
                  pl.BlockSpec((None, P2, NC), g3),
                  pl.BlockSpec((None, P2, NC), g3),
                  pl.BlockSpec((None, NC, 1), g3)],
        out_specs=pl.BlockSpec((nb, NC, nchunk * ls), lambda g: (0, g, 0)),
        out_shape=jax.ShapeDtypeStruct((nb, w, s), BF16),
        scratch_shapes=[pltpu.VMEM((NC * ls, NC * ls), BF16),
                        pltpu.VMEM((NC * ls, P2), BF16),
                        pltpu.VMEM((P2, NC * ls), BF16),
                        pltpu.VMEM((nb, NC, nchunk, ls), BF16)],
        compiler_params=_params("arbitrary"),
        name="s5",
    )(ust, ldt, lamr2, lami2, lamc, bt2, cc2, ca, cb, dcol)


def _tail_kernel(x_ref, mod_ref, hh_ref, som_ref, gain_ref, sga_ref, yst_ref, sgbt_ref, wdown_ref, wglut_ref,
                 wout_ref, g1_ref, b1_ref, x1_ref, h2_ref, *, d, alpha):
    dv = d // MLSTM_HEADS
    tm = x_ref.shape[0]
    sub = tm // TAIL_SUBTILES
    rows = [slice(i * sub, (i + 1) * sub) for i in range(TAIL_SUBTILES)]
    vgt = [jnp.dot(wglut_ref[...], yst_ref[:, r], preferred_element_type=F32) for r in rows]
    ya = [jnp.concatenate(
        [(_standardize(hh_ref[r, h * dv:(h + 1) * dv].astype(F32)) * gain_ref[:, h * dv:(h + 1) * dv]
          * som_ref[r, h * dv:(h + 1) * dv].astype(F32)).astype(BF16) for h in range(MLSTM_HEADS)], axis=1)
        for r in rows]
    y_a = [jnp.dot(ya[i], wdown_ref[...], preferred_element_type=F32) for i in range(TAIL_SUBTILES)]
    z = []
    for i, r in enumerate(rows):
        zbt = vgt[i][0:d, :] * _sigmoid(vgt[i][d:2 * d, :]) * sgbt_ref[:, r].astype(F32)
        z.append((sga_ref[r, :].astype(F32) * y_a[i] + zbt.T).astype(BF16))
    mix = [jnp.dot(z[i], wout_ref[...], preferred_element_type=F32) for i in range(TAIL_SUBTILES)]
    for i, r in enumerate(rows):
        res = alpha * x_ref[r, :] + (1.0 + mod_ref[2:3, :]) * mix[i]
        x1 = _standardize(res) * g1_ref[...] + b1_ref[...]
        x1_ref[r, :] = x1
        h2_ref[r, :] = (_standardize(x1) * (1.0 + mod_ref[4:5, :]) + mod_ref[3:4, :]).astype(BF16)


def _tail(x, mod, hh, som, gain, sga, yst, sgbt, wdown, wglut, wout, g1, b1, *, tm, alpha):
    bsz, s, d = x.shape
    s5w = yst.shape[1]
    tok = lambda b, i: (b, i, 0)
    chan = lambda b, i: (b, 0, i)
    const = lambda b, i: (0, 0)
    return pl.pallas_call(
        functools.partial(_tail_kernel, d=d, alpha=alpha),
        grid=(bsz, s // tm),
        in_specs=[pl.BlockSpec((None, tm, d), tok),
                  pl.BlockSpec((None, 6, d), lambda b, i: (b, 0, 0)),
                  pl.BlockSpec((None, tm, d), tok),
                  pl.BlockSpec((None, tm, d), tok),
                  pl.BlockSpec((1, d), const),
                  pl.BlockSpec((None, tm, d), tok),
                  pl.BlockSpec((None, s5w, tm), chan),
                  pl.BlockSpec((None, d, tm), chan),
                  pl.BlockSpec((d, d), const),
                  pl.BlockSpec((2 * d, s5w), const),
                  pl.BlockSpec((d, d), const),
                  pl.BlockSpec((1, d), const),
                  pl.BlockSpec((1, d), const)],
        out_specs=[pl.BlockSpec((None, tm, d), tok),
                   pl.BlockSpec((None, tm, d), tok)],
        out_shape=[jax.ShapeDtypeStruct((bsz, s, d), F32),
                   jax.ShapeDtypeStruct((bsz, s, d), BF16)],
        compiler_params=_params("arbitrary", "arbitrary"),
        name="tail",
    )(x, mod, hh, som, gain, sga, yst, sgbt, wdown, wglut, wout, g1, b1)


def _ffn_kernel(x1_ref, h2_ref, mod_ref, wup_ref, wconv_ref, bconv_ref, wdown_ref, g2_ref, b2_ref,
                out_ref, gbuf, *, hidden, alpha):
    tm = h2_ref.shape[0]

    @pl.when(pl.program_id(1) == 0)
    def _():
        gbuf[tm:tm + SUBLANES, :] = jnp.zeros((SUBLANES, hidden), F32)

    sub = tm // FFN_SUBTILES
    rows = [slice(i * sub, (i + 1) * sub) for i in range(FFN_SUBTILES)]
    gbuf[0:SUBLANES, :] = gbuf[tm:tm + SUBLANES, :]
    gate = []
    for i, r in enumerate(rows):
        gate.append(jnp.dot(h2_ref[r, :], wup_ref[:, hidden:2 * hidden], preferred_element_type=F32))
        gbuf[SUBLANES + i * sub:SUBLANES + (i + 1) * sub, :] = gate[i]
    act = []
    for i, r in enumerate(rows):
        conv = bconv_ref[...] + wconv_ref[FFN_CONV - 1:FFN_CONV, :] * gate[i]
        for k in range(1, FFN_CONV):
            off = SUBLANES + i * sub - k
            conv = conv + wconv_ref[FFN_CONV - 1 - k:FFN_CONV - k, :] * gbuf[off:off + sub, :]
        gact = _gelu_tanh(conv)
        val = jnp.dot(h2_ref[r, :], wup_ref[:, 0:hidden], preferred_element_type=F32)
        act.append((gact * val).astype(BF16))
    down = [jnp.dot(act[i], wdown_ref[...], preferred_element_type=F32) for i in range(FFN_SUBTILES)]
    for i, r in enumerate(rows):
        res = alpha * x1_ref[r, :] + (1.0 + mod_ref[5:6, :]) * down[i]
        out_ref[r, :] = _standardize(res) * g2_ref[...] + b2_ref[...]


def _ffn(x1, h2, mod, wup, wconv, bconv, wdown, g2, b2, *, tm, alpha):
    bsz, s, d = x1.shape
    hidden = wdown.shape[0]
    tok = lambda b, i: (b, i, 0)
    const = lambda b, i: (0, 0)
    return pl.pallas_call(
        functools.partial(_ffn_kernel, hidden=hidden, alpha=alpha),
        grid=(bsz, s // tm),
        in_specs=[pl.BlockSpec((None, tm, d), tok),
                  pl.BlockSpec((None, tm, d), tok),
                  pl.BlockSpec((None, 6, d), lambda b, i: (b, 0, 0)),
                  pl.BlockSpec((d, 2 * hidden), const),
                  pl.BlockSpec((FFN_CONV, hidden), const),
                  pl.BlockSpec((1, hidden), const),
                  pl.BlockSpec((hidden, d), const),
                  pl.BlockSpec((1, d), const),
                  pl.BlockSpec((1, d), const)],
        out_specs=pl.BlockSpec((None, tm, d), tok),
        out_shape=jax.ShapeDtypeStruct((bsz, s, d), F32),
        scratch_shapes=[pltpu.VMEM((tm + SUBLANES, hidden), F32)],
        compiler_params=_params("arbitrary", "arbitrary"),
        name="ffn",
    )(x1, h2, mod, wup, wconv, bconv, wdown, g2, b2)


def _layer(x, mod, w_in, b_in, w_mlstm_conv, b_mlstm_conv, w_mlstm_q, w_mlstm_k, mlstm_norm_gain,
           w_mlstm_down, s5_lam_re, s5_lam_im, s5_log_dt, s5_b_re, s5_b_im, s5_c_re, s5_c_im, s5_d,
           w_s5_glu, w_mix_out, ln1_gain, ln1_bias, w_ffn_up, w_ffn_conv, b_ffn_conv, w_ffn_down,
           ln2_gain, ln2_bias, *, alpha):
    bsz, s, d = x.shape
    H = MLSTM_HEADS
    s5w = s5_d.shape[0]
    G = s5w // S5_GROUP
    tm = min(512, s)
    chunk = min(256, s)

    o_om, o_ip, o_fp, o_us = d, 2 * d, 2 * d + H, 2 * d + 2 * H
    o_ga, o_gb = o_us + s5w, o_us + s5w + d
    tok_cols = (slice(0, 2 * d), slice(o_ga, o_ga + d))
    chan_cols = (slice(o_us, o_us + s5w), slice(o_gb, o_gb + d), slice(o_ip, o_ip + 2 * H))
    wtok = jnp.concatenate([w_in[:, c].astype(BF16) for c in tok_cols], axis=1)
    btok = jnp.concatenate([b_in[c] for c in tok_cols]).reshape(1, -1)
    wt = jnp.concatenate([w_in[:, c].T.astype(BF16) for c in chan_cols], axis=0)
    bt = jnp.concatenate([b_in[c] for c in chan_cols]).reshape(-1, 1)

    xm, som, sga, ust, sgbt, gt = _inproj(x, mod, wtok, btok, wt, bt, tm=min(1024, s), s5w=s5w)

    hh = _mlstm(xm, gt, w_mlstm_conv, b_mlstm_conv.reshape(1, d), w_mlstm_q.astype(BF16),
                w_mlstm_k.astype(BF16), chunk=chunk)

    dup = lambda a: jnp.concatenate([a, a], axis=-1)
    lamr2 = dup(s5_lam_re)[:, None, :]
    lami2 = dup(s5_lam_im)[:, None, :]
    lamc = jnp.stack([dup(s5_lam_re), dup(s5_lam_im)], axis=1)[..., None]
    bt2 = jnp.concatenate([jnp.swapaxes(s5_b_re, 1, 2), jnp.swapaxes(s5_b_im, 1, 2)], axis=-1)
    cc2 = jnp.stack([dup(s5_c_re), dup(s5_c_im)], axis=1)
    crt = jnp.swapaxes(s5_c_re, 1, 2)
    cit = jnp.swapaxes(s5_c_im, 1, 2)
    ca = jnp.concatenate([crt, -cit], axis=1)
    cb = jnp.concatenate([-cit, -crt], axis=1)
    yst = _s5(ust, s5_log_dt.reshape(G, 1, 1), lamr2, lami2, lamc,
              bt2, cc2, ca, cb, s5_d.reshape(G, S5_GROUP, 1)).reshape(bsz, s5w, s)

    x1, h2 = _tail(x, mod, hh, som, mlstm_norm_gain.reshape(1, d), sga, yst, sgbt,
                   w_mlstm_down.astype(BF16), w_s5_glu.T.astype(BF16),
                   w_mix_out.astype(BF16), ln1_gain.reshape(1, d), ln1_bias.reshape(1, d), tm=tm, alpha=alpha)

    hidden = w_ffn_down.shape[0]
    return _ffn(x1, h2, mod, w_ffn_up.astype(BF16), w_ffn_conv, b_ffn_conv.reshape(1, hidden),
                w_ffn_down.astype(BF16), ln2_gain.reshape(1, d), ln2_bias.reshape(1, d),
                tm=tm, alpha=alpha)


def kernel(x, c, w_ada, b_ada, w_in, b_in, w_mlstm_conv, b_mlstm_conv, w_mlstm_q, w_mlstm_k, mlstm_norm_gain, w_mlstm_down, s5_lam_re, s5_lam_im, s5_log_dt, s5_b_re, s5_b_im, s5_c_re, s5_c_im, s5_d, w_s5_glu, w_mix_out, ln1_gain, ln1_bias, w_ffn_up, w_ffn_conv, b_ffn_conv, w_ffn_down, ln2_gain, ln2_bias):
    depth = w_ada.shape[0]
    alpha = (2.0 * depth) ** 0.25
    bsz, d = c.shape
    for l in range(depth):
        mod = _adaln(c, w_ada[l], b_ada[l]).reshape(bsz, 6, d)
        x = _layer(x, mod, w_in[l], b_in[l], w_mlstm_conv[l], b_mlstm_conv[l], w_mlstm_q[l], w_mlstm_k[l],
                   mlstm_norm_gain[l], w_mlstm_down[l], s5_lam_re[l], s5_lam_im[l], s5_log_dt[l], s5_b_re[l],
                   s5_b_im[l], s5_c_re[l], s5_c_im[l], s5_d[l], w_s5_glu[l], w_mix_out[l], ln1_gain[l],
                   ln1_bias[l], w_ffn_up[l], w_ffn_conv[l], b_ffn_conv[l], w_ffn_down[l], ln2_gain[l],
                   ln2_bias[l], alpha=alpha)
    return x
```
